```python
import math
import jax, jax.numpy as jnp
from jax import lax
import numpy as np

D_MODEL = 2048
BATCH = 4
SEQ = 8192
DEPTH = 4

GRID_W = 64
CTX_LEN = 256
N_EVEN = (DEPTH + 1) // 2
N_ODD = DEPTH // 2
EPS = 1e-6
ADA_SCALE = 0.5
ATTN_WIDTH = D_MODEL // 2
V_DIM = 128
ATTN_HEADS = ATTN_WIDTH // V_DIM
QK_DIM = V_DIM // 2
Q_BLOCK = 128
ROPE_BASE = 10000.0
SGU_WIDTH = D_MODEL - ATTN_WIDTH
SGU_GROUP_DIM = 128
SGU_GROUPS = SGU_WIDTH // SGU_GROUP_DIM
CHUNK = 128
EVEN_IN = 3 * ATTN_WIDTH + 2 * SGU_WIDTH
MIX_WIDTH = ATTN_WIDTH + SGU_WIDTH
LRU_WIDTH = D_MODEL
LRU_BLOCKS = 16
LRU_BLOCK_DIM = LRU_WIDTH // LRU_BLOCKS
CONV_W = 4
LRU_C = 8.0
N_EXPERTS = 32
TOP_K = 4
D_FF = 3 * D_MODEL // 8
SWIGLU_LIMIT = 7.0
SWIGLU_ALPHA = 1.702
MOE_BLOCK = 128

kernel_name = "hybrid_diffattn_sgu_rglru_moe_prefix_dit"

f32 = jnp.float32


def rmsnorm(x, g):
    xf = x.astype(f32)
    y = xf * lax.rsqrt(jnp.mean(xf * xf, axis=-1, keepdims=True) + EPS)
    return (y * g.astype(f32)).astype(x.dtype)


def modulate(h, shift, scale):
    return h * (1.0 + scale) + shift


def axial_rope_tables(n):
    rows = n // GRID_W
    row = jnp.repeat(jnp.arange(rows, dtype=f32), GRID_W)
    col = jnp.tile(jnp.arange(GRID_W, dtype=f32), rows)
    axis_dim = QK_DIM // 2
    inv_freq = ROPE_BASE ** (-jnp.arange(0, axis_dim, 2, dtype=f32) / axis_dim)
    ang_r = (row[:, None] * inv_freq)[:, None, None, :]
    ang_c = (col[:, None] * inv_freq)[:, None, None, :]
    return (jnp.cos(ang_r), jnp.sin(ang_r), jnp.cos(ang_c), jnp.sin(ang_c))


def rope_half(t, cos, sin):
    t1, t2 = jnp.split(t, 2, axis=-1)
    return jnp.concatenate([t1 * cos - t2 * sin, t2 * cos + t1 * sin], axis=-1)


def apply_axial_rope(t, rope):
    cos_r, sin_r, cos_c, sin_c = (r.astype(t.dtype) for r in rope)
    t_row, t_col = jnp.split(t, 2, axis=-1)
    return jnp.concatenate([rope_half(t_row, cos_r, sin_r), rope_half(t_col, cos_c, sin_c)], axis=-1)


def diff_weights(q, k, lam):
    s = jnp.einsum('bqhcd,bkhcd->bhcqk', q, k).astype(f32) * (QK_DIM ** -0.5)
    p = jax.nn.softmax(s, axis=-1)
    return p[:, :, 0] - lam * p[:, :, 1]


def diff_attention_latent(q_l, k_l, v_l, k_c, v_c, lam):
    B, N = q_l.shape[:2]
    nb = N // Q_BLOCK
    k_all = jnp.concatenate([k_l, k_c], axis=1)
    v_all = jnp.concatenate([v_l, v_c], axis=1)
    q_blocks = jnp.moveaxis(q_l.reshape(B, nb, Q_BLOCK, ATTN_HEADS, 2, QK_DIM), 1, 0)

    def block(qb):
        w = diff_weights(qb, k_all, lam)
        return jnp.einsum('bhqk,bkhv->bqhv', w.astype(v_all.dtype), v_all)

    o = lax.map(block, q_blocks)
    return jnp.moveaxis(o, 0, 1).reshape(B, N, ATTN_HEADS, V_DIM)


def diff_attention_ctx(q_c, k_c, v_c, lam):
    w = diff_weights(q_c, k_c, lam)
    return jnp.einsum('bhqk,bkhv->bqhv', w.astype(v_c.dtype), v_c)


def spatial_gating(z, sgu_g, w_s, b_s):
    u, v = jnp.split(z, 2, axis=-1)
    v = rmsnorm(v, sgu_g)
    B, N = v.shape[:2]
    vc = v.reshape(B, N // CHUNK, CHUNK, SGU_GROUPS, SGU_GROUP_DIM)
    mixed = jnp.einsum('gpq,bnqgc->bnpgc', w_s, vc) + b_s.T[:, :, None]
    return u * mixed.reshape(B, N, SGU_WIDTH)


def even_mixer(h_lat, h_ctx, rope, lam_init, need_ctx, w_in, q_g, k_g, lq1, lk1, lq2, lk2,
               subln_g, sgu_g, w_s, b_s, w_out):
    lam = (jnp.exp(jnp.sum(lq1.astype(f32) * lk1.astype(f32)))
           - jnp.exp(jnp.sum(lq2.astype(f32) * lk2.astype(f32))) + lam_init)

    def project(h, with_rope):
        B, N = h.shape[:2]
        q, k, v, z = jnp.split(h @ w_in, [ATTN_WIDTH, 2 * ATTN_WIDTH, 3 * ATTN_WIDTH], axis=-1)
        q = rmsnorm(q.reshape(B, N, ATTN_HEADS, 2, QK_DIM), q_g)
        k = rmsnorm(k.reshape(B, N, ATTN_HEADS, 2, QK_DIM), k_g)
        if with_rope:
            q = apply_axial_rope(q, rope)
            k = apply_axial_rope(k, rope)
        return q, k, v.reshape(B, N, ATTN_HEADS, V_DIM), jax.nn.gelu(z)

    def merge(o, z):
        B, N = o.shape[:2]
        o = rmsnorm(o, subln_g) * (1.0 - lam_init)
        s = spatial_gating(z, sgu_g, w_s, b_s)
        return jnp.concatenate([o.reshape(B, N, ATTN_WIDTH), s], axis=-1) @ w_out

    q_l, k_l, v_l, z_l = project(h_lat, True)
    q_c, k_c, v_c, z_c = project(h_ctx, False)
    y_lat = merge(diff_attention_latent(q_l, k_l, v_l, k_c, v_c, lam), z_l)
    y_ctx = merge(diff_attention_ctx(q_c, k_c, v_c, lam), z_c) if need_ctx else None
    return y_lat, y_ctx


def centred_conv(x, w, b):
    n = x.shape[1]
    left = CONV_W // 2
    xp = jnp.pad(x, ((0, 0), (left, CONV_W - 1 - left), (0, 0)))
    return sum(xp[:, k:k + n] * w[k] for k in range(CONV_W)) + b


def lru_coeffs(x, w_a, b_a, w_i, b_i, lam):
    B, N = x.shape[:2]
    xb = x.reshape(B, N, LRU_BLOCKS, LRU_BLOCK_DIM)
    r = jax.nn.sigmoid(jnp.einsum('bnhi,hij->bnhj', xb, w_a.astype(f32)) + b_a.astype(f32))
    i = jax.nn.sigmoid(jnp.einsum('bnhi,hij->bnhj', xb, w_i.astype(f32)) + b_i.astype(f32))
    r = r.reshape(B, N, LRU_WIDTH)
    i = i.reshape(B, N, LRU_WIDTH)
    log_a = -LRU_C * r * jax.nn.softplus(-lam.astype(f32))
    a = jnp.exp(log_a)
    return a, jnp.sqrt(-jnp.expm1(2.0 * log_a)) * i * x


def linear_scan(a, b, reverse):
    def combine(left, right):
        a1, b1 = left
        a2, b2 = right
        return a1 * a2, a2 * b1 + b2
    return lax.associative_scan(combine, (a, b), axis=1, reverse=reverse)


def odd_mixer(h_lat, h_ctx, need_ctx, w_in, conv_w, conv_b, w_a, b_a, w_i, b_i, lam, w_out):
    def branches(h):
        xr, gr = jnp.split(h @ w_in, 2, axis=-1)
        return centred_conv(xr, conv_w, conv_b).astype(f32), jax.nn.gelu(gr)

    x_l, g_l = branches(h_lat)
    x_c, g_c = branches(h_ctx)
    y_l = jnp.zeros_like(x_l)
    y_c = jnp.zeros_like(x_c)
    for d, rev in enumerate((False, True)):
        a_c, b_c = lru_coeffs(x_c, w_a[d], b_a[d], w_i[d], b_i[d], lam[d])
        _, h_c = linear_scan(a_c, b_c, rev)
        h0 = h_c[:, 0] if rev else h_c[:, -1]
        a_l, b_l = lru_coeffs(x_l, w_a[d], b_a[d], w_i[d], b_i[d], lam[d])
        a_cum, h_l = linear_scan(a_l, b_l, rev)
        y_l = y_l + h_l + a_cum * h0[:, None, :]
        y_c = y_c + h_c
    y_lat = (y_l.astype(h_lat.dtype) * g_l) @ w_out
    y_ctx = (y_c.astype(h_ctx.dtype) * g_c) @ w_out if need_ctx else None
    return y_lat, y_ctx


def swiglu_clamped(h):
    h_glu, h_lin = h[..., ::2], h[..., 1::2]
    h_glu = jnp.minimum(h_glu, SWIGLU_LIMIT)
    h_lin = jnp.clip(h_lin, -SWIGLU_LIMIT, SWIGLU_LIMIT)
    return h_glu * jax.nn.sigmoid(SWIGLU_ALPHA * h_glu) * (h_lin + 1.0)


def moe_ffn(tok, w_r, b_r, w1, b1, w2, b2):
    n_tok = tok.shape[0]
    n_assign = n_tok * TOP_K
    logits = (tok @ w_r + b_r).astype(f32)
    top_val, top_idx = lax.top_k(logits, TOP_K)
    gate = jax.nn.softmax(top_val, axis=-1).reshape(-1)
    expert = top_idx.reshape(-1)
    order = jnp.argsort(expert)
    e_sorted = expert[order]
    counts = jnp.bincount(expert, length=N_EXPERTS)
    padded = (counts + MOE_BLOCK - 1) // MOE_BLOCK * MOE_BLOCK
    pad_end = jnp.cumsum(padded)
    pad_start = pad_end - padded
    grp_start = jnp.cumsum(counts) - counts
    dest = pad_start[e_sorted] + jnp.arange(n_assign) - grp_start[e_sorted]
    n_rows = n_assign + N_EXPERTS * MOE_BLOCK
    n_blocks = n_rows // MOE_BLOCK
    row_tok = jnp.zeros((n_rows,), jnp.int32).at[dest].set((order // TOP_K).astype(jnp.int32))
    row_gate = jnp.zeros((n_rows,), f32).at[dest].set(gate[order])
    block_expert = jnp.minimum(
        jnp.searchsorted(pad_end, jnp.arange(n_blocks) * MOE_BLOCK, side='right'), N_EXPERTS - 1)

    def expert_block(args):
        toks, gts, e = args
        hb = swiglu_clamped(tok[toks] @ w1[e] + b1[e])
        return (hb @ w2[e] + b2[e]) * gts[:, None].astype(tok.dtype)

    y = lax.map(expert_block, (row_tok.reshape(n_blocks, MOE_BLOCK),
                               row_gate.reshape(n_blocks, MOE_BLOCK), block_expert))
    return jax.ops.segment_sum(y.reshape(n_rows, -1), row_tok, num_segments=n_tok)


def setup_inputs(seed: int = 0) -> dict:
    key = jax.random.key(seed)
    ks = iter(jax.random.split(key, 64))
    D = D_MODEL

    def nrm(shape, scale):
        return jax.random.normal(next(ks), shape, f32) * scale

    def gain(shape):
        return 1.0 + nrm(shape, 0.02)

    a0 = jax.random.uniform(next(ks), (N_ODD, 2, LRU_WIDTH), f32, 0.9, 0.999)
    p = a0 ** (1.0 / LRU_C)
    od_lam = jnp.log(p) - jnp.log1p(-p)
    return {
        "x": nrm((BATCH, SEQ, D), 1.0),
        "c": nrm((BATCH, D), 1.0),
        "ctx": nrm((BATCH, CTX_LEN, D), 1.0),
        "c_ctx": nrm((D,), 1.0),
        "ada_w": nrm((DEPTH, D, 6 * D), ADA_SCALE * D ** -0.5),
        "ada_b": nrm((DEPTH, 6 * D), 0.02),
        "norm_mix_g": gain((DEPTH, D)),
        "norm_ffn_g": gain((DEPTH, D)),
        "ev_w_in": nrm((N_EVEN, D, EVEN_IN), D ** -0.5),
        "ev_q_g": gain((N_EVEN, QK_DIM)),
        "ev_k_g": gain((N_EVEN, QK_DIM)),
        "ev_lq1": nrm((N_EVEN, QK_DIM), 0.1),
        "ev_lk1": nrm((N_EVEN, QK_DIM), 0.1),
        "ev_lq2": nrm((N_EVEN, QK_DIM), 0.1),
        "ev_lk2": nrm((N_EVEN, QK_DIM), 0.1),
        "ev_subln_g": gain((N_EVEN, V_DIM)),
        "ev_sgu_g": gain((N_EVEN, SGU_WIDTH)),
        "ev_w_s": nrm((N_EVEN, SGU_GROUPS, CHUNK, CHUNK), CHUNK ** -0.5),
        "ev_b_s": gain((N_EVEN, SGU_GROUPS, CHUNK)),
        "ev_w_out": nrm((N_EVEN, MIX_WIDTH, D), MIX_WIDTH ** -0.5),
        "od_w_in": nrm((N_ODD, D, 2 * LRU_WIDTH), D ** -0.5),
        "od_conv_w": nrm((N_ODD, CONV_W, LRU_WIDTH), CONV_W ** -0.5),
        "od_conv_b": nrm((N_ODD, LRU_WIDTH), 0.02),
        "od_w_a": nrm((N_ODD, 2, LRU_BLOCKS, LRU_BLOCK_DIM, LRU_BLOCK_DIM), LRU_BLOCK_DIM ** -0.5),
        "od_b_a": nrm((N_ODD, 2, LRU_BLOCKS, LRU_BLOCK_DIM), 0.02),
        "od_w_i": nrm((N_ODD, 2, LRU_BLOCKS, LRU_BLOCK_DIM, LRU_BLOCK_DIM), LRU_BLOCK_DIM ** -0.5),
        "od_b_i": nrm((N_ODD, 2, LRU_BLOCKS, LRU_BLOCK_DIM), 0.02),
        "od_lam": od_lam,
        "od_w_out": nrm((N_ODD, LRU_WIDTH, D), LRU_WIDTH ** -0.5),
        "moe_w_r": nrm((DEPTH, D, N_EXPERTS), D ** -0.5),
        "moe_b_r": nrm((DEPTH, N_EXPERTS), 0.01),
        "moe_w1": nrm((DEPTH, N_EXPERTS, D, 2 * D_FF), D ** -0.5),
        "moe_b1": nrm((DEPTH, N_EXPERTS, 2 * D_FF), 0.02),
        "moe_w2": nrm((DEPTH, N_EXPERTS, D_FF, D), D_FF ** -0.5),
        "moe_b2": nrm((DEPTH, N_EXPERTS, D), 0.02),
    }


def reference(x, c, ctx, c_ctx, ada_w, ada_b, norm_mix_g, norm_ffn_g,
              ev_w_in, ev_q_g, ev_k_g, ev_lq1, ev_lk1, ev_lq2, ev_lk2, ev_subln_g, ev_sgu_g,
              ev_w_s, ev_b_s, ev_w_out,
              od_w_in, od_conv_w, od_conv_b, od_w_a, od_b_a, od_w_i, od_b_i, od_lam, od_w_out,
              moe_w_r, moe_b_r, moe_w1, moe_b1, moe_w2, moe_b2):
    B, N, D = x.shape
    C = ctx.shape[1]
    rope = axial_rope_tables(N)
    silu_c = jax.nn.silu(c)
    silu_cc = jax.nn.silu(c_ctx)
    for l in range(DEPTH):
        need_ctx = l < DEPTH - 1
        mod_l = (silu_c @ ada_w[l] + ada_b[l])[:, None, :]
        mod_c = silu_cc @ ada_w[l] + ada_b[l]
        sh1, sc1, g1, sh2, sc2, g2 = jnp.split(mod_l, 6, axis=-1)
        csh1, csc1, cg1, csh2, csc2, cg2 = jnp.split(mod_c, 6, axis=-1)

        h_lat = modulate(rmsnorm(x, norm_mix_g[l]), sh1, sc1)
        h_ctx = modulate(rmsnorm(ctx, norm_mix_g[l]), csh1, csc1)
        if l % 2 == 0:
            e = l // 2
            lam_init = 0.8 - 0.6 * math.exp(-0.3 * l)
            y_lat, y_ctx = even_mixer(h_lat, h_ctx, rope, lam_init, need_ctx, ev_w_in[e], ev_q_g[e],
                                      ev_k_g[e], ev_lq1[e], ev_lk1[e], ev_lq2[e], ev_lk2[e],
                                      ev_subln_g[e], ev_sgu_g[e], ev_w_s[e], ev_b_s[e], ev_w_out[e])
        else:
            o = l // 2
            y_lat, y_ctx = odd_mixer(h_lat, h_ctx, need_ctx, od_w_in[o], od_conv_w[o], od_conv_b[o],
                                     od_w_a[o], od_b_a[o], od_w_i[o], od_b_i[o], od_lam[o], od_w_out[o])
        x = x + g1 * y_lat

        f_in_lat = modulate(rmsnorm(x, norm_ffn_g[l]), sh2, sc2).reshape(B * N, D)
        if need_ctx:
            ctx = ctx + cg1 * y_ctx
            f_in_ctx = modulate(rmsnorm(ctx, norm_ffn_g[l]), csh2, csc2).reshape(B * C, D)
            f = moe_ffn(jnp.concatenate([f_in_ctx, f_in_lat], axis=0), moe_w_r[l], moe_b_r[l],
                        moe_w1[l], moe_b1[l], moe_w2[l], moe_b2[l])
            ctx = ctx + cg2 * f[:B * C].reshape(B, C, D)
            f_lat = f[B * C:].reshape(B, N, D)
        else:
            f_lat = moe_ffn(f_in_lat, moe_w_r[l], moe_b_r[l], moe_w1[l], moe_b1[l],
                            moe_w2[l], moe_b2[l]).reshape(B, N, D)
        x = x + g2 * f_lat
    return x
```

```python
import functools
import math

import jax
import jax.numpy as jnp
from jax import lax
from jax.experimental import pallas as pl
from jax.experimental.pallas import tpu as pltpu

f32 = jnp.float32
bf16 = jnp.bfloat16
u32 = jnp.uint32
i32 = jnp.int32

EPS = 1e-6
GRID_W = 64
HEAD_DIM = 128
QK_DIM = 64
ROPE_BASE = 10000.0
CHUNK = 128
LRU_BLOCK = 128
CONV_W = 4
LRU_C = 8.0
TOP_K = 4
SWIGLU_LIMIT = 7.0
SWIGLU_ALPHA = 1.702

LANES = 128
SUBLANES = 8
BF16_ROWS = 16
VMEM_LIMIT = 56 * 1024 * 1024

TOK_TILE = 256
MOE_ROWS = 256
KV_TILE = 256


def _cparams(*sem):
    return pltpu.CompilerParams(dimension_semantics=sem, vmem_limit_bytes=VMEM_LIMIT)


def _pick(n, options):
    for o in options:
        if n % o == 0:
            return o
    raise ValueError(f"no tile for {n} in {options}")


def _gelu(x):
    return 0.5 * x * (1.0 + jnp.tanh(math.sqrt(2.0 / math.pi) * (x + 0.044715 * (x * x * x))))


def _rms_mod(x, g, shift, scale):
    y = x * lax.rsqrt(jnp.mean(x * x, axis=-1, keepdims=True) + EPS) * g
    return y * (1.0 + scale) + shift


def _pack_halves(y):
    half = y.shape[1] // 2
    yb = y.astype(bf16).astype(f32)
    lo = lax.shift_right_logical(lax.bitcast_convert_type(yb[:, :half], u32), jnp.uint32(16))
    hi = lax.bitcast_convert_type(yb[:, half:], u32) & jnp.uint32(0xFFFF0000)
    return lo | hi


def _unpack_halves(u):
    lo = lax.bitcast_convert_type(lax.shift_left(u, jnp.uint32(16)), f32)
    hi = lax.bitcast_convert_type(u & jnp.uint32(0xFFFF0000), f32)
    return lo, hi


def _ada_kernel(c_ref, w_ref, b_ref, o_ref):
    cv = c_ref[...]
    s = (cv * jax.nn.sigmoid(cv)).astype(bf16)
    o_ref[0] = jnp.dot(s, w_ref[0].astype(bf16), preferred_element_type=f32) + b_ref[0]


def _ada_all(cvec, ada_w, ada_b):
    depth, d, n6 = ada_w.shape
    rows = cvec.shape[0]
    tn = _pick(n6, (1024, 512, 256, 128))
    return pl.pallas_call(
        _ada_kernel,
        grid=(depth, n6 // tn),
        in_specs=[
            pl.BlockSpec((rows, d), lambda l, j: (0, 0)),
            pl.BlockSpec((1, d, tn), lambda l, j: (l, 0, j)),
            pl.BlockSpec((1, 1, tn), lambda l, j: (l, 0, j)),
        ],
        out_specs=pl.BlockSpec((1, rows, tn), lambda l, j: (l, 0, j)),
        out_shape=jax.ShapeDtypeStruct((depth, rows, n6), f32),
        compiler_params=_cparams("arbitrary", "arbitrary"),
        name="ada_mod",
    )(cvec, ada_w, ada_b.reshape(depth, 1, n6))


def _norm_mod_kernel(x_ref, g_ref, mod_ref, h_ref, *, shift_row, scale_row):
    m = mod_ref[0, 0]
    h = _rms_mod(x_ref[...], g_ref[...], m[shift_row:shift_row + 1], m[scale_row:scale_row + 1])
    h_ref[...] = h.astype(h_ref.dtype)


def _norm_mod(xs, g, mod, layer, mod_row, shift_row, scale_row):
    t, d = xs.shape
    return pl.pallas_call(
        functools.partial(_norm_mod_kernel, shift_row=shift_row, scale_row=scale_row),
        grid=(t // TOK_TILE,),
        in_specs=[
            pl.BlockSpec((TOK_TILE, d), lambda i: (i, 0)),
            pl.BlockSpec((1, d), lambda i: (0, 0)),
            pl.BlockSpec((1, 1, 6, d), lambda i: (layer, mod_row(i), 0, 0)),
        ],
        out_specs=pl.BlockSpec((TOK_TILE, d), lambda i: (i, 0)),
        out_shape=jax.ShapeDtypeStruct((t, d), bf16),
        compiler_params=_cparams("arbitrary"),
        name="norm_mod",
    )(xs, g.reshape(1, d), mod)


def _matmul_kernel(a_ref, w_ref, o_ref):
    o_ref[...] = jnp.dot(a_ref[...], w_ref[...], preferred_element_type=f32).astype(o_ref.dtype)


def _matmul(a, w, out_dtype=bf16):
    m, k = a.shape
    n = w.shape[1]
    tm = _pick(m, (1024, 768, 512, 256))
    tn = _pick(n, (1024, 512, 256, 128))
    return pl.pallas_call(
        _matmul_kernel,
        grid=(m // tm, n // tn),
        in_specs=[
            pl.BlockSpec((tm, k), lambda i, j: (i, 0)),
            pl.BlockSpec((k, tn), lambda i, j: (0, j)),
        ],
        out_specs=pl.BlockSpec((tm, tn), lambda i, j: (i, j)),
        out_shape=jax.ShapeDtypeStruct((m, n), out_dtype),
        compiler_params=_cparams("arbitrary", "arbitrary"),
        name="in_proj",
    )(a, w)


def _outproj_kernel(*refs, n_a, gate_row):
    a_refs = refs[:n_a]
    w_ref, x_ref, mod_ref, o_ref = refs[n_a:]
    acc = None
    off = 0
    for a in a_refs:
        ka = a.shape[1]
        part = jnp.dot(a[...], w_ref[off:off + ka, :], preferred_element_type=f32)
        acc = part if acc is None else acc + part
        off += ka
    gate = mod_ref[0, 0][gate_row:gate_row + 1]
    o_ref[...] = x_ref[...] + gate * acc


def _outproj_residual(a_list, w, xs, mod, layer, mod_row, gate_row):
    t, d = xs.shape
    kw = w.shape[0]
    n_a = len(a_list)
    in_specs = [pl.BlockSpec((TOK_TILE, a.shape[1]), lambda i: (i, 0)) for a in a_list]
    in_specs += [
        pl.BlockSpec((kw, d), lambda i: (0, 0)),
        pl.BlockSpec((TOK_TILE, d), lambda i: (i, 0)),
        pl.BlockSpec((1, 1, 6, d), lambda i: (layer, mod_row(i), 0, 0)),
    ]
    return pl.pallas_call(
        functools.partial(_outproj_kernel, n_a=n_a, gate_row=gate_row),
        grid=(t // TOK_TILE,),
        in_specs=in_specs,
        out_specs=pl.BlockSpec((TOK_TILE, d), lambda i: (i, 0)),
        out_shape=jax.ShapeDtypeStruct((t, d), f32),
        input_output_aliases={n_a + 1: 0},
        compiler_params=_cparams("arbitrary"),
        name="out_proj",
    )(*a_list, w, xs, mod)


def _qk_kernel(q_ref, k_ref, qg_ref, kg_ref, cos_ref, sa_ref, sb_ref, qo_ref, ko_ref, *, heads):
    lane = lax.broadcasted_iota(i32, (TOK_TILE, LANES), 1)
    first = lane < QK_DIM
    cos_t = cos_ref[...]
    sin_a = sa_ref[...]
    sin_b = sb_ref[...]

    def one(src, g_ref, dst, scale):
        g = g_ref[...]
        for h in range(heads):
            cols = slice(h * HEAD_DIM, (h + 1) * HEAD_DIM)
            x = src[:, cols].astype(f32)
            sq = x * x
            s0 = jnp.sum(jnp.where(first, sq, 0.0), axis=-1, keepdims=True)
            s1 = jnp.sum(jnp.where(first, 0.0, sq), axis=-1, keepdims=True)
            ms = jnp.where(first, s0, s1) * (1.0 / QK_DIM)
            y = x * lax.rsqrt(ms + EPS) * g
            y = y * cos_t + pltpu.roll(y, LANES - 16, 1) * sin_a + pltpu.roll(y, 16, 1) * sin_b
            dst[:, cols] = (y * scale).astype(dst.dtype)

    one(q_ref, qg_ref, qo_ref, QK_DIM ** -0.5)
    one(k_ref, kg_ref, ko_ref, 1.0)


def _qk_norm_rope(proj, q_g, k_g, rope, tiles_per_sample, aw):
    t = proj.shape[0]
    heads = aw // HEAD_DIM
    cos_t, sin_a, sin_b = rope
    g2 = lambda g: jnp.concatenate([g, g]).reshape(1, LANES).astype(f32)
    tab = pl.BlockSpec((TOK_TILE, LANES), lambda i: (i % tiles_per_sample, 0))
    return pl.pallas_call(
        functools.partial(_qk_kernel, heads=heads),
        grid=(t // TOK_TILE,),
        in_specs=[
            pl.BlockSpec((TOK_TILE, aw), lambda i: (i, 0)),
            pl.BlockSpec((TOK_TILE, aw), lambda i: (i, 1)),
            pl.BlockSpec((1, LANES), lambda i: (0, 0)),
            pl.BlockSpec((1, LANES), lambda i: (0, 0)),
            tab, tab, tab,
        ],
        out_specs=[pl.BlockSpec((TOK_TILE, aw), lambda i: (i, 0))] * 2,
        out_shape=[jax.ShapeDtypeStruct((t, aw), bf16)] * 2,
        compiler_params=_cparams("arbitrary"),
        name="qk_norm_rope",
    )(proj, proj, g2(q_g), g2(k_g), cos_t, sin_a, sin_b)


def _rope_tables(n_lat, n_ctx):
    tpos = jnp.arange(n_lat)
    row = (tpos // GRID_W).astype(f32)
    col = (tpos % GRID_W).astype(f32)
    axis_dim = QK_DIM // 2
    nfreq = axis_dim // 2
    inv_freq = ROPE_BASE ** (-jnp.arange(0, axis_dim, 2, dtype=f32) / axis_dim)
    lane = jnp.arange(LANES)
    is_col = ((lane % QK_DIM) // axis_dim) == 1
    second = ((lane % axis_dim) // nfreq) == 1
    fr = inv_freq[lane % nfreq]
    pos = jnp.where(is_col[None, :], col[:, None], row[:, None])
    ang = pos * fr[None, :]
    cos_t = jnp.cos(ang)
    sin = jnp.sin(ang)
    sin_a = jnp.where(second[None, :], 0.0, -sin)
    sin_b = jnp.where(second[None, :], sin, 0.0)
    pad = lambda a, v: jnp.concatenate([jnp.full((n_ctx, LANES), v, f32), a], axis=0)
    return pad(cos_t, 1.0), pad(sin_a, 0.0), pad(sin_b, 0.0)


def _attn_kernel(q_ref, k_ref, v_ref, lp_ref, sg_ref, o_ref, qs_ref, m_ref, l_ref, acc_ref,
                 *, n_ctx_q, ctx_len, seq_all, lam_init):
    tq = TOK_TILE
    tk = KV_TILE
    qi = pl.program_id(2)
    lane = lax.broadcasted_iota(i32, (tq, LANES), 1)
    q = q_ref[...]
    zero = jnp.zeros_like(q)
    qs_ref[0:tq, :] = jnp.where(lane < QK_DIM, q, zero)
    qs_ref[tq:2 * tq, :] = jnp.where(lane < QK_DIM, zero, q)
    m_ref[...] = jnp.full(m_ref.shape, -jnp.inf, f32)
    l_ref[...] = jnp.zeros(l_ref.shape, f32)
    acc_ref[...] = jnp.zeros(acc_ref.shape, f32)
    n_kv = jnp.where(qi < n_ctx_q, ctx_len // tk, seq_all // tk)

    def body(j, carry):
        off = pl.multiple_of(j * tk, tk)
        kt = k_ref[pl.ds(off, tk), :]
        vt = v_ref[pl.ds(off, tk), :]
        s = lax.dot_general(qs_ref[...], kt, (((1,), (1,)), ((), ())), preferred_element_type=f32)
        m_prev = m_ref[...]
        m_new = jnp.maximum(m_prev, jnp.max(s, axis=-1, keepdims=True))
        alpha = jnp.exp(m_prev - m_new)
        p = jnp.exp(s - m_new)
        l_ref[...] = alpha * l_ref[...] + jnp.sum(p, axis=-1, keepdims=True)
        acc_ref[...] = alpha * acc_ref[...] + jnp.dot(p.astype(bf16), vt, preferred_element_type=f32)
        m_ref[...] = m_new
        return carry

    lax.fori_loop(0, n_kv, body, 0)

    lp = lp_ref[...]
    lam = (jnp.exp(jnp.sum(lp[0:1] * lp[1:2], axis=-1, keepdims=True))
           - jnp.exp(jnp.sum(lp[2:3] * lp[3:4], axis=-1, keepdims=True)) + lam_init)
    o1 = acc_ref[0:tq, :] / l_ref[0:tq, :]
    o2 = acc_ref[tq:2 * tq, :] / l_ref[tq:2 * tq, :]
    o = o1 - lam * o2
    y = o * lax.rsqrt(jnp.mean(o * o, axis=-1, keepdims=True) + EPS) * sg_ref[...]
    o_ref[...] = (y * (1.0 - lam_init)).astype(o_ref.dtype)


def _diff_attention(q, k, v_src, v_col0, lam_params, subln_g, batch, s_all, n_ctx, lam_init):
    t, aw = q.shape
    heads = aw // HEAD_DIM
    nq = s_all // TOK_TILE
    kern = functools.partial(_attn_kernel, n_ctx_q=n_ctx // TOK_TILE, ctx_len=n_ctx, seq_all=s_all,
                             lam_init=lam_init)
    return pl.pallas_call(
        kern,
        grid=(batch, heads, nq),
        in_specs=[
            pl.BlockSpec((TOK_TILE, HEAD_DIM), lambda b, h, i: (b * nq + i, h)),
            pl.BlockSpec((s_all, HEAD_DIM), lambda b, h, i: (b, h)),
            pl.BlockSpec((s_all, HEAD_DIM), lambda b, h, i: (b, v_col0 + h)),
            pl.BlockSpec((4, QK_DIM), lambda b, h, i: (0, 0)),
            pl.BlockSpec((1, HEAD_DIM), lambda b, h, i: (0, 0)),
        ],
        out_specs=pl.BlockSpec((TOK_TILE, HEAD_DIM), lambda b, h, i: (b * nq + i, h)),
        out_shape=jax.ShapeDtypeStruct((t, aw), bf16),
        scratch_shapes=[
            pltpu.VMEM((2 * TOK_TILE, HEAD_DIM), bf16),
            pltpu.VMEM((2 * TOK_TILE, 1), f32),
            pltpu.VMEM((2 * TOK_TILE, 1), f32),
            pltpu.VMEM((2 * TOK_TILE, HEAD_DIM), f32),
        ],
        compiler_params=_cparams("arbitrary", "arbitrary", "arbitrary"),
        name="diff_attention",
    )(q, k, v_src, lam_params, subln_g.reshape(1, HEAD_DIM))


def _sgu_kernel(u_ref, v_ref, g_ref, ws_ref, bs_ref, o_ref, *, groups):
    v = _gelu(v_ref[...].astype(f32))
    vn = v * lax.rsqrt(jnp.mean(v * v, axis=-1, keepdims=True) + EPS) * g_ref[...]
    vb = vn.astype(bf16)
    for g in range(groups):
        cols = slice(g * CHUNK, (g + 1) * CHUNK)
        bias = bs_ref[:, g:g + 1]
        for c in range(TOK_TILE // CHUNK):
            rows = slice(c * CHUNK, (c + 1) * CHUNK)
            mixed = jnp.dot(ws_ref[g], vb[rows, cols], preferred_element_type=f32) + bias
            u = _gelu(u_ref[rows, cols].astype(f32))
            o_ref[rows, cols] = (u * mixed).astype(o_ref.dtype)


def _sgu(proj, u_blk, sgu_g, w_s, b_s, sw):
    t = proj.shape[0]
    groups = sw // CHUNK
    return pl.pallas_call(
        functools.partial(_sgu_kernel, groups=groups),
        grid=(t // TOK_TILE,),
        in_specs=[
            pl.BlockSpec((TOK_TILE, sw), lambda i: (i, u_blk)),
            pl.BlockSpec((TOK_TILE, sw), lambda i: (i, u_blk + 1)),
            pl.BlockSpec((1, sw), lambda i: (0, 0)),
            pl.BlockSpec((groups, CHUNK, CHUNK), lambda i: (0, 0, 0)),
            pl.BlockSpec((CHUNK, groups), lambda i: (0, 0)),
        ],
        out_specs=pl.BlockSpec((TOK_TILE, sw), lambda i: (i, 0)),
        out_shape=jax.ShapeDtypeStruct((t, sw), bf16),
        compiler_params=_cparams("arbitrary"),
        name="spatial_gating",
    )(proj, proj, sgu_g.reshape(1, sw), w_s.astype(bf16), b_s.T.astype(f32))


def _lru_kernel(xr_ref, gr_ref, cw_ref, cb_ref, wai_ref, bai_ref, lam_ref, o_ref,
                xc_ref, y_ref, a_ref, b_ref, *, n_tiles, n_ctx_tiles):
    rt = TOK_TILE
    s_all = n_tiles * rt
    row = lax.broadcasted_iota(i32, (rt, LANES), 0)
    row8 = lax.broadcasted_iota(i32, (SUBLANES, LANES), 0)
    cw = cw_ref[...]
    cbias = cb_ref[...]

    def conv_tile(j, carry):
        r0 = pl.multiple_of(j * rt, rt)
        x0 = xr_ref[pl.ds(r0, rt), :].astype(f32)
        seg_first = jnp.logical_or(j == 0, j == n_ctx_tiles)
        seg_last = jnp.logical_or(j == n_ctx_tiles - 1, j == n_tiles - 1)
        rp = pl.multiple_of(jnp.maximum(r0 - BF16_ROWS, 0), BF16_ROWS)
        rn = pl.multiple_of(jnp.minimum(r0 + rt, s_all - BF16_ROWS), BF16_ROWS)
        prev = xr_ref[pl.ds(rp, BF16_ROWS), :].astype(f32)
        nxt = xr_ref[pl.ds(rn, BF16_ROWS), :].astype(f32)
        pm = jnp.where(seg_first, 0.0, 1.0)
        nm = jnp.where(seg_last, 0.0, 1.0)
        p1 = prev[BF16_ROWS - 1:BF16_ROWS] * pm
        p2 = prev[BF16_ROWS - 2:BF16_ROWS - 1] * pm
        n0 = nxt[0:1] * nm
        xm1 = jnp.where(row == 0, p1, pltpu.roll(x0, 1, 0))
        xm2 = jnp.where(row == 0, p2, jnp.where(row == 1, p1, pltpu.roll(x0, 2, 0)))
        xp1 = jnp.where(row == rt - 1, n0, pltpu.roll(x0, rt - 1, 0))
        xc_ref[pl.ds(r0, rt), :] = (cw[0:1] * xm2 + cw[1:2] * xm1 + cw[2:3] * x0 + cw[3:4] * xp1 + cbias)
        return carry

    lax.fori_loop(0, n_tiles, conv_tile, 0)

    def scan_dir(d, reverse):
        z = -lam_ref[d:d + 1, :]
        sp = jnp.maximum(z, 0.0) + jnp.log1p(jnp.exp(-jnp.abs(z)))
        w = wai_ref[d, 0]
        bias = bai_ref[d, 0]

        def tile_body(step, carry):
            if reverse:
                j = jnp.where(step < n_ctx_tiles, n_ctx_tiles - 1 - step, n_tiles - 1 - (step - n_ctx_tiles))
            else:
                j = step
            r0 = pl.multiple_of(j * rt, rt)
            xc = xc_ref[pl.ds(r0, rt), :]
            ri = jnp.dot(xc.astype(bf16), w, preferred_element_type=f32) + bias
            r = jax.nn.sigmoid(ri[:, :LRU_BLOCK])
            gi = jax.nn.sigmoid(ri[:, LRU_BLOCK:])
            a = jnp.exp(-LRU_C * r * sp)
            a_ref[...] = a
            b_ref[...] = jnp.sqrt(1.0 - a * a) * gi * xc

            def group(gidx, c):
                g = (rt // SUBLANES - 1 - gidx) if reverse else gidx
                o8 = pl.multiple_of(g * SUBLANES, SUBLANES)
                av = a_ref[pl.ds(o8, SUBLANES), :]
                bv = b_ref[pl.ds(o8, SUBLANES), :]
                for s in (1, 2, 4):
                    if reverse:
                        keep = row8 < SUBLANES - s
                        sh = SUBLANES - s
                    else:
                        keep = row8 >= s
                        sh = s
                    a_sh = jnp.where(keep, pltpu.roll(av, sh, 0), 1.0)
                    b_sh = jnp.where(keep, pltpu.roll(bv, sh, 0), 0.0)
                    bv = av * b_sh + bv
                    av = av * a_sh
                h = bv + av * c
                rows = pl.ds(pl.multiple_of(r0 + o8, SUBLANES), SUBLANES)
                if d == 0:
                    y_ref[rows, :] = h
                else:
                    y_ref[rows, :] = y_ref[rows, :] + h
                return h[0:1] if reverse else h[SUBLANES - 1:SUBLANES]

            return lax.fori_loop(0, rt // SUBLANES, group, carry)

        lax.fori_loop(0, n_tiles, tile_body, jnp.zeros((1, LANES), f32))

    scan_dir(0, False)
    scan_dir(1, True)

    def out_tile(j, carry):
        rows = pl.ds(pl.multiple_of(j * rt, rt), rt)
        o_ref[rows, :] = (y_ref[rows, :] * _gelu(gr_ref[rows, :].astype(f32))).astype(o_ref.dtype)
        return carry

    lax.fori_loop(0, n_tiles, out_tile, 0)


def _lru(proj, conv_w, conv_b, w_ai, b_ai, lam, batch, s_all, n_ctx, lw):
    t = proj.shape[0]
    ncb = lw // LRU_BLOCK
    kern = functools.partial(_lru_kernel, n_tiles=s_all // TOK_TILE, n_ctx_tiles=n_ctx // TOK_TILE)
    return pl.pallas_call(
        kern,
        grid=(batch, ncb),
        in_specs=[
            pl.BlockSpec((s_all, LRU_BLOCK), lambda b, c: (b, c)),
            pl.BlockSpec((s_all, LRU_BLOCK), lambda b, c: (b, ncb + c)),
            pl.BlockSpec((CONV_W, LRU_BLOCK), lambda b, c: (0, c)),
            pl.BlockSpec((1, LRU_BLOCK), lambda b, c: (0, c)),
            pl.BlockSpec((2, 1, LRU_BLOCK, 2 * LRU_BLOCK), lambda b, c: (0, c, 0, 0)),
            pl.BlockSpec((2, 1, 1, 2 * LRU_BLOCK), lambda b, c: (0, c, 0, 0)),
            pl.BlockSpec((2, LRU_BLOCK), lambda b, c: (0, c)),
        ],
        out_specs=pl.BlockSpec((s_all, LRU_BLOCK), lambda b, c: (b, c)),
        out_shape=jax.ShapeDtypeStruct((t, lw), bf16),
        scratch_shapes=[
            pltpu.VMEM((s_all, LRU_BLOCK), f32),
            pltpu.VMEM((s_all, LRU_BLOCK), f32),
            pltpu.VMEM((TOK_TILE, LRU_BLOCK), f32),
            pltpu.VMEM((TOK_TILE, LRU_BLOCK), f32),
        ],
        compiler_params=_cparams("arbitrary", "arbitrary"),
        name="rg_lru",
    )(proj, proj, conv_w, conv_b.reshape(1, lw), w_ai, b_ai, lam)


def _route_kernel(x_ref, g_ref, mod_ref, wr_ref, br_ref, tri_ref,
                  f_ref, idx_ref, gate_ref, rank_ref, cnt_ref, carry_ref, *, shift_row, scale_row):
    @pl.when(pl.program_id(0) == 0)
    def _():
        carry_ref[...] = jnp.zeros(carry_ref.shape, f32)

    m = mod_ref[0, 0]
    h = _rms_mod(x_ref[...], g_ref[...], m[shift_row:shift_row + 1], m[scale_row:scale_row + 1])
    f_ref[...] = _pack_halves(h)
    logits = jnp.dot(h.astype(bf16), wr_ref[...], preferred_element_type=f32) + br_ref[...]

    lane = lax.broadcasted_iota(i32, logits.shape, 1)
    vals = logits
    sels, tops, idxs = [], [], []
    for _ in range(TOP_K):
        mx = jnp.max(vals, axis=-1, keepdims=True)
        ik = jnp.min(jnp.where(vals == mx, lane, LANES), axis=-1, keepdims=True)
        sel = lane == ik
        sels.append(sel)
        tops.append(mx)
        idxs.append(ik)
        vals = jnp.where(sel, -jnp.inf, vals)
    es = [jnp.exp(tv - tops[0]) for tv in tops]
    denom = es[0] + es[1] + es[2] + es[3]

    onehot = jnp.zeros(logits.shape, f32)
    for sel in sels:
        onehot = onehot + jnp.where(sel, 1.0, 0.0)
    before = jnp.dot(tri_ref[...], onehot.astype(bf16), preferred_element_type=f32) + carry_ref[...]

    idx_o = jnp.zeros(logits.shape, i32)
    gate_o = jnp.zeros(logits.shape, f32)
    rank_o = jnp.zeros(logits.shape, i32)
    for k in range(TOP_K):
        rk = jnp.sum(jnp.where(sels[k], before, 0.0), axis=-1, keepdims=True).astype(i32)
        idx_o = jnp.where(lane == k, idxs[k], idx_o)
        gate_o = jnp.where(lane == k, es[k] / denom, gate_o)
        rank_o = jnp.where(lane == k, rk, rank_o)
    idx_ref[...] = idx_o
    gate_ref[...] = gate_o
    rank_ref[...] = rank_o
    carry = carry_ref[...] + jnp.sum(onehot, axis=0, keepdims=True)
    carry_ref[...] = carry
    cnt_ref[...] = jnp.broadcast_to(carry, cnt_ref.shape).astype(i32)


def _route(xs, g, mod, layer, mod_row, shift_row, scale_row, w_r, b_r):
    t, d = xs.shape
    n_exp = w_r.shape[1]
    wr = jnp.zeros((d, LANES), bf16).at[:, :n_exp].set(w_r.astype(bf16))
    br = jnp.full((1, LANES), -1e30, f32).at[0, :n_exp].set(b_r.astype(f32))
    r = jnp.arange(TOK_TILE)
    tri = (r[:, None] > r[None, :]).astype(bf16)
    tile = lambda w: pl.BlockSpec((TOK_TILE, w), lambda i: (i, 0))
    return pl.pallas_call(
        functools.partial(_route_kernel, shift_row=shift_row, scale_row=scale_row),
        grid=(t // TOK_TILE,),
        in_specs=[
            tile(d),
            pl.BlockSpec((1, d), lambda i: (0, 0)),
            pl.BlockSpec((1, 1, 6, d), lambda i: (layer, mod_row(i), 0, 0)),
            pl.BlockSpec((d, LANES), lambda i: (0, 0)),
            pl.BlockSpec((1, LANES), lambda i: (0, 0)),
            pl.BlockSpec((TOK_TILE, TOK_TILE), lambda i: (0, 0)),
        ],
        out_specs=[tile(d // 2), tile(LANES), tile(LANES), tile(LANES),
                   pl.BlockSpec((SUBLANES, LANES), lambda i: (0, 0))],
        out_shape=[
            jax.ShapeDtypeStruct((t, d // 2), u32),
            jax.ShapeDtypeStruct((t, LANES), i32),
            jax.ShapeDtypeStruct((t, LANES), f32),
            jax.ShapeDtypeStruct((t, LANES), i32),
            jax.ShapeDtypeStruct((SUBLANES, LANES), i32),
        ],
        scratch_shapes=[pltpu.VMEM((1, LANES), f32)],
        compiler_params=_cparams("arbitrary"),
        name="moe_route",
    )(xs, g.reshape(1, d), mod, wr, br, tri)


def _row_gather(idx_ref, src_hbm, buf, sem, n):
    def issue(j, carry):
        r = idx_ref[0, 0, j]
        pltpu.make_async_copy(src_hbm.at[pl.ds(r, 1), :], buf.at[pl.ds(j, 1), :], sem).start()
        return carry

    lax.fori_loop(0, n, issue, 0)
    pltpu.make_async_copy(src_hbm.at[pl.ds(0, n), :], buf, sem).wait()


def _expert_kernel(be_ref, nu_ref, tok_ref, f_hbm, w1g_ref, w1l_ref, b1g_ref, b1l_ref, w2_ref, b2_ref,
                   y_ref, xbuf, sem):
    i = pl.program_id(0)

    @pl.when(i < nu_ref[0])
    def _():
        _row_gather(tok_ref, f_hbm, xbuf, sem, MOE_ROWS)
        lo, hi = _unpack_halves(xbuf[...])
        lo = lo.astype(bf16)
        hi = hi.astype(bf16)
        half = lo.shape[1]

        def proj(w_ref, b_ref):
            return (jnp.dot(lo, w_ref[0, :half, :], preferred_element_type=f32)
                    + jnp.dot(hi, w_ref[0, half:, :], preferred_element_type=f32) + b_ref[0])

        h_glu = jnp.minimum(proj(w1g_ref, b1g_ref), SWIGLU_LIMIT)
        h_lin = jnp.clip(proj(w1l_ref, b1l_ref), -SWIGLU_LIMIT, SWIGLU_LIMIT)
        act = h_glu * jax.nn.sigmoid(SWIGLU_ALPHA * h_glu) * (h_lin + 1.0)
        y = jnp.dot(act.astype(bf16), w2_ref[0], preferred_element_type=f32) + b2_ref[0]
        y_ref[...] = _pack_halves(y)

    @pl.when(i >= nu_ref[0])
    def _():
        y_ref[...] = jnp.zeros(y_ref.shape, y_ref.dtype)


def _experts(block_expert, n_used, row_tok, fpk, w1g, w1l, b1g, b1l, w2, b2):
    n_blocks = block_expert.shape[0]
    t, half = fpk.shape
    n_exp, d, dff = w1g.shape
    wspec = lambda shape: pl.BlockSpec((1,) + shape, lambda i, be, nu: (be[i], 0, 0))
    grid_spec = pltpu.PrefetchScalarGridSpec(
        num_scalar_prefetch=2,
        grid=(n_blocks,),
        in_specs=[
            pl.BlockSpec((1, 1, MOE_ROWS), lambda i, be, nu: (i, 0, 0), memory_space=pltpu.SMEM),
            pl.BlockSpec(memory_space=pl.ANY),
            wspec((d, dff)), wspec((d, dff)), wspec((1, dff)), wspec((1, dff)),
            wspec((dff, d)), wspec((1, d)),
        ],
        out_specs=pl.BlockSpec((MOE_ROWS, half), lambda i, be, nu: (i, 0)),
        scratch_shapes=[pltpu.VMEM((MOE_ROWS, half), u32), pltpu.SemaphoreType.DMA(())],
    )
    return pl.pallas_call(
        _expert_kernel,
        grid_spec=grid_spec,
        out_shape=jax.ShapeDtypeStruct((n_blocks * MOE_ROWS, half), u32),
        compiler_params=_cparams("arbitrary"),
        name="moe_experts",
    )(block_expert, n_used, row_tok.reshape(n_blocks, 1, MOE_ROWS), fpk, w1g, w1l, b1g, b1l, w2, b2)


def _combine_kernel(dest_ref, x_ref, gate_ref, mod_ref, y_hbm, o_ref, buf, sem, *, gate_row):
    _row_gather(dest_ref, y_hbm, buf, sem, TOP_K * TOK_TILE)
    half = buf.shape[1]
    acc_lo = jnp.zeros((TOK_TILE, half), f32)
    acc_hi = jnp.zeros((TOK_TILE, half), f32)
    for k in range(TOP_K):
        lo, hi = _unpack_halves(buf[k * TOK_TILE:(k + 1) * TOK_TILE, :])
        gk = gate_ref[:, k:k + 1]
        acc_lo = acc_lo + gk * lo
        acc_hi = acc_hi + gk * hi
    g2 = mod_ref[0, 0][gate_row:gate_row + 1]
    o_ref[:, :half] = x_ref[:, :half] + g2[:, :half] * acc_lo
    o_ref[:, half:] = x_ref[:, half:] + g2[:, half:] * acc_hi


def _combine(dest_tiles, xs, gates, mod, layer, mod_row, gate_row, ypk):
    t, d = xs.shape
    return pl.pallas_call(
        functools.partial(_combine_kernel, gate_row=gate_row),
        grid=(t // TOK_TILE,),
        in_specs=[
            pl.BlockSpec((1, 1, TOP_K * TOK_TILE), lambda i: (i, 0, 0), memory_space=pltpu.SMEM),
            pl.BlockSpec((TOK_TILE, d), lambda i: (i, 0)),
            pl.BlockSpec((TOK_TILE, LANES), lambda i: (i, 0)),
            pl.BlockSpec((1, 1, 6, d), lambda i: (layer, mod_row(i), 0, 0)),
            pl.BlockSpec(memory_space=pl.ANY),
        ],
        out_specs=pl.BlockSpec((TOK_TILE, d), lambda i: (i, 0)),
        out_shape=jax.ShapeDtypeStruct((t, d), f32),
        input_output_aliases={1: 0},
        scratch_shapes=[pltpu.VMEM((TOP_K * TOK_TILE, d // 2), u32), pltpu.SemaphoreType.DMA(())],
        compiler_params=_cparams("arbitrary"),
        name="moe_combine",
    )(dest_tiles, xs, gates, mod, ypk)


def _moe_layer(xs, g, mod, layer, mod_row, w_r, b_r, w1g, w1l, b1g, b1l, w2, b2):
    t, d = xs.shape
    n_exp = w_r.shape[1]
    fpk, idx_o, gate_o, rank_o, cnt = _route(xs, g, mod, layer, mod_row, 3, 4, w_r, b_r)
    counts = cnt[0, :n_exp]
    padded = (counts + MOE_ROWS - 1) // MOE_ROWS * MOE_ROWS
    pad_end = jnp.cumsum(padded)
    pad_start = pad_end - padded
    dest = pad_start[idx_o[:, :TOP_K]] + rank_o[:, :TOP_K]
    n_rows = t * TOP_K + n_exp * MOE_ROWS
    n_blocks = n_rows // MOE_ROWS
    tok = jnp.broadcast_to(jnp.arange(t, dtype=i32)[:, None], (t, TOP_K))
    row_tok = jnp.zeros((n_rows,), i32).at[dest.reshape(-1)].set(tok.reshape(-1))
    block_expert = jnp.minimum(
        jnp.searchsorted(pad_end, jnp.arange(n_blocks, dtype=i32) * MOE_ROWS, side="right"),
        n_exp - 1).astype(i32)
    n_used = (pad_end[-1:] // MOE_ROWS).astype(i32)
    ypk = _experts(block_expert, n_used, row_tok, fpk, w1g, w1l, b1g, b1l, w2, b2)
    nt = t // TOK_TILE
    dest_tiles = dest.reshape(nt, TOK_TILE, TOP_K).transpose(0, 2, 1).reshape(nt, 1, TOP_K * TOK_TILE)
    return _combine(dest_tiles.astype(i32), xs, gate_o, mod, layer, mod_row, 5, ypk)


def kernel(x, c, ctx, c_ctx, ada_w, ada_b, norm_mix_g, norm_ffn_g, ev_w_in, ev_q_g, ev_k_g, ev_lq1, ev_lk1, ev_lq2, ev_lk2, ev_subln_g, ev_sgu_g, ev_w_s, ev_b_s, ev_w_out, od_w_in, od_conv_w, od_conv_b, od_w_a, od_b_a, od_w_i, od_b_i, od_lam, od_w_out, moe_w_r, moe_b_r, moe_w1, moe_b1, moe_w2, moe_b2):
    batch, n_lat, d = x.shape
    n_ctx = ctx.shape[1]
    depth = ada_w.shape[0]
    s_all = n_ctx + n_lat
    assert n_ctx % TOK_TILE == 0 and n_lat % TOK_TILE == 0 and n_lat % GRID_W == 0
    assert d % (2 * HEAD_DIM) == 0 and batch < SUBLANES
    tiles_per_sample = s_all // TOK_TILE
    ctx_tiles = n_ctx // TOK_TILE
    aw = d // 2
    sw = d - aw
    lw = od_w_out.shape[1]

    def mod_row(i):
        return jnp.where(i % tiles_per_sample < ctx_tiles, batch, i // tiles_per_sample)

    cvec = jnp.zeros((SUBLANES, d), f32).at[:batch].set(c).at[batch].set(c_ctx)
    mod = _ada_all(cvec, ada_w, ada_b).reshape(depth, SUBLANES, 6, d)
    rope = _rope_tables(n_lat, n_ctx)
    xs = jnp.concatenate([ctx, x], axis=1).reshape(batch * s_all, d)

    for l in range(depth):
        h = _norm_mod(xs, norm_mix_g[l], mod, l, mod_row, 0, 1)
        if l % 2 == 0:
            e = l // 2
            lam_init = 0.8 - 0.6 * math.exp(-0.3 * l)
            proj = _matmul(h, ev_w_in[e].astype(bf16))
            q, k = _qk_norm_rope(proj, ev_q_g[e], ev_k_g[e], rope, tiles_per_sample, aw)
            lam_params = jnp.stack([ev_lq1[e], ev_lk1[e], ev_lq2[e], ev_lk2[e]]).astype(f32)
            o = _diff_attention(q, k, proj, 2 * aw // HEAD_DIM, lam_params, ev_subln_g[e],
                                batch, s_all, n_ctx, lam_init)
            sg = _sgu(proj, 3 * aw // sw, ev_sgu_g[e], ev_w_s[e], ev_b_s[e], sw)
            xs = _outproj_residual([o, sg], ev_w_out[e].astype(bf16), xs, mod, l, mod_row, 2)
        else:
            o_ = l // 2
            proj = _matmul(h, od_w_in[o_].astype(bf16))
            w_ai = jnp.concatenate([od_w_a[o_], od_w_i[o_]], axis=-1).astype(bf16)
            b_ai = jnp.concatenate([od_b_a[o_], od_b_i[o_]], axis=-1)[:, :, None, :].astype(f32)
            y = _lru(proj, od_conv_w[o_], od_conv_b[o_], w_ai, b_ai, od_lam[o_], batch, s_all, n_ctx, lw)
            xs = _outproj_residual([y], od_w_out[o_].astype(bf16), xs, mod, l, mod_row, 2)
        w1 = moe_w1[l]
        b1 = moe_b1[l]
        xs = _moe_layer(
            xs, norm_ffn_g[l], mod, l, mod_row, moe_w_r[l], moe_b_r[l],
            w1[:, :, 0::2].astype(bf16), w1[:, :, 1::2].astype(bf16),
            b1[:, None, 0::2], b1[:, None, 1::2],
            moe_w2[l].astype(bf16), moe_b2[l][:, None, :])
    return xs.reshape(batch, s_all, d)[:, n_ctx:, :]
```

```python
import functools
import math

import jax
import jax.numpy as jnp
from jax import lax
from jax.experimental import pallas as pl
from jax.experimental.pallas import tpu as pltpu

f32 = jnp.float32
bf16 = jnp.bfloat16
u32 = jnp.uint32
i32 = jnp.int32

EPS = 1e-6
GRID_W = 64
HEAD_DIM = 128
QK_DIM = 64
ROPE_BASE = 10000.0
CHUNK = 128
LRU_BLOCK = 128
CONV_W = 4
LRU_C = 8.0
TOP_K = 4
SWIGLU_LIMIT = 7.0
SWIGLU_ALPHA = 1.702

LANES = 128
SUBLANES = 8
BF16_ROWS = 16
VMEM_LIMIT = 56 * 1024 * 1024

TOK_TILE = 256
MOE_ROWS = 256
KV_TILE = 512
KV_CTX_TILE = 256


def _cparams(*sem):
    return pltpu.CompilerParams(dimension_semantics=sem, vmem_limit_bytes=VMEM_LIMIT)


def _pick(n, options):
    for o in options:
        if n % o == 0:
            return o
    raise ValueError(f"no tile for {n} in {options}")


def _gelu(x):
    return 0.5 * x * (1.0 + jnp.tanh(math.sqrt(2.0 / math.pi) * (x + 0.044715 * (x * x * x))))


def _rms_mod(x, g, shift, scale):
    y = x * lax.rsqrt(jnp.mean(x * x, axis=-1, keepdims=True) + EPS) * g
    return y * (1.0 + scale) + shift


def _pack_halves(y):
    half = y.shape[1] // 2
    yb = y.astype(bf16).astype(f32)
    lo = lax.shift_right_logical(lax.bitcast_convert_type(yb[:, :half], u32), jnp.uint32(16))
    hi = lax.bitcast_convert_type(yb[:, half:], u32) & jnp.uint32(0xFFFF0000)
    return lo | hi


def _unpack_halves(u):
    lo = lax.bitcast_convert_type(lax.shift_left(u, jnp.uint32(16)), f32)
    hi = lax.bitcast_convert_type(u & jnp.uint32(0xFFFF0000), f32)
    return lo, hi


def _ada_kernel(c_ref, w_ref, b_ref, o_ref):
    cv = c_ref[...]
    s = (cv * jax.nn.sigmoid(cv)).astype(bf16)
    o_ref[0] = jnp.dot(s, w_ref[0].astype(bf16), preferred_element_type=f32) + b_ref[0]


def _ada_all(cvec, ada_w, ada_b):
    depth, d, n6 = ada_w.shape
    rows = cvec.shape[0]
    tn = _pick(n6, (1024, 512, 256, 128))
    return pl.pallas_call(
        _ada_kernel,
        grid=(depth, n6 // tn),
        in_specs=[
            pl.BlockSpec((rows, d), lambda l, j: (0, 0)),
            pl.BlockSpec((1, d, tn), lambda l, j: (l, 0, j)),
            pl.BlockSpec((1, 1, tn), lambda l, j: (l, 0, j)),
        ],
        out_specs=pl.BlockSpec((1, rows, tn), lambda l, j: (l, 0, j)),
        out_shape=jax.ShapeDtypeStruct((depth, rows, n6), f32),
        compiler_params=_cparams("arbitrary", "arbitrary"),
        name="ada_mod",
    )(cvec, ada_w, ada_b.reshape(depth, 1, n6))


def _norm_mod_kernel(x_ref, g_ref, mod_ref, h_ref, *, shift_row, scale_row):
    m = mod_ref[0, 0]
    h = _rms_mod(x_ref[...], g_ref[...], m[shift_row:shift_row + 1], m[scale_row:scale_row + 1])
    h_ref[...] = h.astype(h_ref.dtype)


def _norm_mod(xs, g, mod, layer, mod_row, shift_row, scale_row):
    t, d = xs.shape
    return pl.pallas_call(
        functools.partial(_norm_mod_kernel, shift_row=shift_row, scale_row=scale_row),
        grid=(t // TOK_TILE,),
        in_specs=[
            pl.BlockSpec((TOK_TILE, d), lambda i: (i, 0)),
            pl.BlockSpec((1, d), lambda i: (0, 0)),
            pl.BlockSpec((1, 1, 6, d), lambda i: (layer, mod_row(i), 0, 0)),
        ],
        out_specs=pl.BlockSpec((TOK_TILE, d), lambda i: (i, 0)),
        out_shape=jax.ShapeDtypeStruct((t, d), bf16),
        compiler_params=_cparams("arbitrary"),
        name="norm_mod",
    )(xs, g.reshape(1, d), mod)


def _matmul_kernel(a_ref, w_ref, o_ref):
    o_ref[...] = jnp.dot(a_ref[...], w_ref[...], preferred_element_type=f32).astype(o_ref.dtype)


def _matmul(a, w, out_dtype=bf16):
    m, k = a.shape
    n = w.shape[1]
    tm = _pick(m, (1024, 768, 512, 256))
    tn = _pick(n, (1024, 512, 256, 128))
    return pl.pallas_call(
        _matmul_kernel,
        grid=(m // tm, n // tn),
        in_specs=[
            pl.BlockSpec((tm, k), lambda i, j: (i, 0)),
            pl.BlockSpec((k, tn), lambda i, j: (0, j)),
        ],
        out_specs=pl.BlockSpec((tm, tn), lambda i, j: (i, j)),
        out_shape=jax.ShapeDtypeStruct((m, n), out_dtype),
        compiler_params=_cparams("arbitrary", "arbitrary"),
        name="in_proj",
    )(a, w)


def _outproj_kernel(*refs, n_a, gate_row):
    a_refs = refs[:n_a]
    w_ref, x_ref, mod_ref, o_ref = refs[n_a:]
    acc = None
    off = 0
    for a in a_refs:
        ka = a.shape[1]
        part = jnp.dot(a[...], w_ref[off:off + ka, :], preferred_element_type=f32)
        acc = part if acc is None else acc + part
        off += ka
    gate = mod_ref[0, 0][gate_row:gate_row + 1]
    o_ref[...] = x_ref[...] + gate * acc


def _outproj_residual(a_list, w, xs, mod, layer, mod_row, gate_row):
    t, d = xs.shape
    kw = w.shape[0]
    n_a = len(a_list)
    in_specs = [pl.BlockSpec((TOK_TILE, a.shape[1]), lambda i: (i, 0)) for a in a_list]
    in_specs += [
        pl.BlockSpec((kw, d), lambda i: (0, 0)),
        pl.BlockSpec((TOK_TILE, d), lambda i: (i, 0)),
        pl.BlockSpec((1, 1, 6, d), lambda i: (layer, mod_row(i), 0, 0)),
    ]
    return pl.pallas_call(
        functools.partial(_outproj_kernel, n_a=n_a, gate_row=gate_row),
        grid=(t // TOK_TILE,),
        in_specs=in_specs,
        out_specs=pl.BlockSpec((TOK_TILE, d), lambda i: (i, 0)),
        out_shape=jax.ShapeDtypeStruct((t, d), f32),
        input_output_aliases={n_a + 1: 0},
        compiler_params=_cparams("arbitrary"),
        name="out_proj",
    )(*a_list, w, xs, mod)


def _qk_kernel(q_ref, k_ref, qg_ref, kg_ref, cos_ref, sa_ref, sb_ref, qo_ref, ko_ref, *, heads):
    lane = lax.broadcasted_iota(i32, (TOK_TILE, LANES), 1)
    first = lane < QK_DIM
    cos_t = cos_ref[...]
    sin_a = sa_ref[...]
    sin_b = sb_ref[...]

    def one(src, g_ref, dst, scale):
        g = g_ref[...]
        for h in range(heads):
            cols = slice(h * HEAD_DIM, (h + 1) * HEAD_DIM)
            x = src[:, cols].astype(f32)
            sq = x * x
            s0 = jnp.sum(jnp.where(first, sq, 0.0), axis=-1, keepdims=True)
            s1 = jnp.sum(jnp.where(first, 0.0, sq), axis=-1, keepdims=True)
            ms = jnp.where(first, s0, s1) * (1.0 / QK_DIM)
            y = x * lax.rsqrt(ms + EPS) * g
            y = y * cos_t + pltpu.roll(y, LANES - 16, 1) * sin_a + pltpu.roll(y, 16, 1) * sin_b
            dst[:, cols] = (y * scale).astype(dst.dtype)

    one(q_ref, qg_ref, qo_ref, QK_DIM ** -0.5)
    one(k_ref, kg_ref, ko_ref, 1.0)


def _qk_norm_rope(proj, q_g, k_g, rope, tiles_per_sample, aw):
    t = proj.shape[0]
    heads = aw // HEAD_DIM
    cos_t, sin_a, sin_b = rope
    g2 = lambda g: jnp.concatenate([g, g]).reshape(1, LANES).astype(f32)
    tab = pl.BlockSpec((TOK_TILE, LANES), lambda i: (i % tiles_per_sample, 0))
    return pl.pallas_call(
        functools.partial(_qk_kernel, heads=heads),
        grid=(t // TOK_TILE,),
        in_specs=[
            pl.BlockSpec((TOK_TILE, aw), lambda i: (i, 0)),
            pl.BlockSpec((TOK_TILE, aw), lambda i: (i, 1)),
            pl.BlockSpec((1, LANES), lambda i: (0, 0)),
            pl.BlockSpec((1, LANES), lambda i: (0, 0)),
            tab, tab, tab,
        ],
        out_specs=[pl.BlockSpec((TOK_TILE, aw), lambda i: (i, 0))] * 2,
        out_shape=[jax.ShapeDtypeStruct((t, aw), bf16)] * 2,
        compiler_params=_cparams("arbitrary"),
        name="qk_norm_rope",
    )(proj, proj, g2(q_g), g2(k_g), cos_t, sin_a, sin_b)


def _rope_tables(n_lat, n_ctx):
    tpos = jnp.arange(n_lat)
    row = (tpos // GRID_W).astype(f32)
    col = (tpos % GRID_W).astype(f32)
    axis_dim = QK_DIM // 2
    nfreq = axis_dim // 2
    inv_freq = ROPE_BASE ** (-jnp.arange(0, axis_dim, 2, dtype=f32) / axis_dim)
    lane = jnp.arange(LANES)
    is_col = ((lane % QK_DIM) // axis_dim) == 1
    second = ((lane % axis_dim) // nfreq) == 1
    fr = inv_freq[lane % nfreq]
    pos = jnp.where(is_col[None, :], col[:, None], row[:, None])
    ang = pos * fr[None, :]
    cos_t = jnp.cos(ang)
    sin = jnp.sin(ang)
    sin_a = jnp.where(second[None, :], 0.0, -sin)
    sin_b = jnp.where(second[None, :], sin, 0.0)
    pad = lambda a, v: jnp.concatenate([jnp.full((n_ctx, LANES), v, f32), a], axis=0)
    return pad(cos_t, 1.0), pad(sin_a, 0.0), pad(sin_b, 0.0)


def _attn_kernel(q_ref, k_ref, v_ref, lp_ref, sg_ref, o_ref, qs_ref, m_ref, l_ref, acc_ref,
                 *, n_ctx_q, ctx_len, seq_all, lam_init):
    tq = TOK_TILE
    qi = pl.program_id(2)
    lane = lax.broadcasted_iota(i32, (tq, LANES), 1)
    q = q_ref[...]
    zero = jnp.zeros_like(q)
    qs_ref[0:tq, :] = jnp.where(lane < QK_DIM, q, zero)
    qs_ref[tq:2 * tq, :] = jnp.where(lane < QK_DIM, zero, q)
    m_ref[...] = jnp.full(m_ref.shape, -jnp.inf, f32)
    l_ref[...] = jnp.zeros(l_ref.shape, f32)
    acc_ref[...] = jnp.zeros(acc_ref.shape, f32)
    def kv_step(off, tk):
        kt = k_ref[pl.ds(off, tk), :]
        vt = v_ref[pl.ds(off, tk), :]
        s = lax.dot_general(qs_ref[...], kt, (((1,), (1,)), ((), ())), preferred_element_type=f32)
        m_prev = m_ref[...]
        m_new = jnp.maximum(m_prev, jnp.max(s, axis=-1, keepdims=True))
        alpha = jnp.exp(m_prev - m_new)
        p = jnp.exp(s - m_new)
        l_ref[...] = alpha * l_ref[...] + jnp.sum(p, axis=-1, keepdims=True)
        acc_ref[...] = alpha * acc_ref[...] + jnp.dot(p.astype(bf16), vt, preferred_element_type=f32)
        m_ref[...] = m_new

    def ctx_body(j, carry):
        kv_step(pl.multiple_of(j * KV_CTX_TILE, KV_CTX_TILE), KV_CTX_TILE)
        return carry

    lax.fori_loop(0, ctx_len // KV_CTX_TILE, ctx_body, 0)

    def lat_body(j, carry):
        kv_step(pl.multiple_of(ctx_len + j * KV_TILE, KV_CTX_TILE), KV_TILE)
        return carry

    lax.fori_loop(0, jnp.where(qi < n_ctx_q, 0, (seq_all - ctx_len) // KV_TILE), lat_body, 0)

    lp = lp_ref[...]
    lam = (jnp.exp(jnp.sum(lp[0:1] * lp[1:2], axis=-1, keepdims=True))
           - jnp.exp(jnp.sum(lp[2:3] * lp[3:4], axis=-1, keepdims=True)) + lam_init)
    o1 = acc_ref[0:tq, :] / l_ref[0:tq, :]
    o2 = acc_ref[tq:2 * tq, :] / l_ref[tq:2 * tq, :]
    o = o1 - lam * o2
    y = o * lax.rsqrt(jnp.mean(o * o, axis=-1, keepdims=True) + EPS) * sg_ref[...]
    o_ref[...] = (y * (1.0 - lam_init)).astype(o_ref.dtype)


def _diff_attention(q, k, v_src, v_col0, lam_params, subln_g, batch, s_all, n_ctx, lam_init):
    t, aw = q.shape
    heads = aw // HEAD_DIM
    nq = s_all // TOK_TILE
    kern = functools.partial(_attn_kernel, n_ctx_q=n_ctx // TOK_TILE, ctx_len=n_ctx, seq_all=s_all,
                             lam_init=lam_init)
    return pl.pallas_call(
        kern,
        grid=(batch, heads, nq),
        in_specs=[
            pl.BlockSpec((TOK_TILE, HEAD_DIM), lambda b, h, i: (b * nq + i, h)),
            pl.BlockSpec((s_all, HEAD_DIM), lambda b, h, i: (b, h)),
            pl.BlockSpec((s_all, HEAD_DIM), lambda b, h, i: (b, v_col0 + h)),
            pl.BlockSpec((4, QK_DIM), lambda b, h, i: (0, 0)),
            pl.BlockSpec((1, HEAD_DIM), lambda b, h, i: (0, 0)),
        ],
        out_specs=pl.BlockSpec((TOK_TILE, HEAD_DIM), lambda b, h, i: (b * nq + i, h)),
        out_shape=jax.ShapeDtypeStruct((t, aw), bf16),
        scratch_shapes=[
            pltpu.VMEM((2 * TOK_TILE, HEAD_DIM), bf16),
            pltpu.VMEM((2 * TOK_TILE, 1), f32),
            pltpu.VMEM((2 * TOK_TILE, 1), f32),
            pltpu.VMEM((2 * TOK_TILE, HEAD_DIM), f32),
        ],
        compiler_params=_cparams("arbitrary", "arbitrary", "arbitrary"),
        name="diff_attention",
    )(q, k, v_src, lam_params, subln_g.reshape(1, HEAD_DIM))


def _sgu_kernel(u_ref, v_ref, g_ref, ws_ref, bs_ref, o_ref, *, groups):
    v = _gelu(v_ref[...].astype(f32))
    vn = v * lax.rsqrt(jnp.mean(v * v, axis=-1, keepdims=True) + EPS) * g_ref[...]
    vb = vn.astype(bf16)
    for g in range(groups):
        cols = slice(g * CHUNK, (g + 1) * CHUNK)
        bias = bs_ref[:, g:g + 1]
        for c in range(TOK_TILE // CHUNK):
            rows = slice(c * CHUNK, (c + 1) * CHUNK)
            mixed = jnp.dot(ws_ref[g], vb[rows, cols], preferred_element_type=f32) + bias
            u = _gelu(u_ref[rows, cols].astype(f32))
            o_ref[rows, cols] = (u * mixed).astype(o_ref.dtype)


def _sgu(proj, u_blk, sgu_g, w_s, b_s, sw):
    t = proj.shape[0]
    groups = sw // CHUNK
    return pl.pallas_call(
        functools.partial(_sgu_kernel, groups=groups),
        grid=(t // TOK_TILE,),
        in_specs=[
            pl.BlockSpec((TOK_TILE, sw), lambda i: (i, u_blk)),
            pl.BlockSpec((TOK_TILE, sw), lambda i: (i, u_blk + 1)),
            pl.BlockSpec((1, sw), lambda i: (0, 0)),
            pl.BlockSpec((groups, CHUNK, CHUNK), lambda i: (0, 0, 0)),
            pl.BlockSpec((CHUNK, groups), lambda i: (0, 0)),
        ],
        out_specs=pl.BlockSpec((TOK_TILE, sw), lambda i: (i, 0)),
        out_shape=jax.ShapeDtypeStruct((t, sw), bf16),
        compiler_params=_cparams("arbitrary"),
        name="spatial_gating",
    )(proj, proj, sgu_g.reshape(1, sw), w_s.astype(bf16), b_s.T.astype(f32))


def _lru_kernel(xr_ref, gr_ref, cw_ref, cb_ref, wai_ref, bai_ref, lam_ref, o_ref,
                xc_ref, y_ref, a_ref, b_ref, *, n_tiles, n_ctx_tiles):
    rt = TOK_TILE
    s_all = n_tiles * rt
    row = lax.broadcasted_iota(i32, (rt, LANES), 0)
    row8 = lax.broadcasted_iota(i32, (SUBLANES, LANES), 0)
    cw = cw_ref[...]
    cbias = cb_ref[...]

    def conv_tile(j, carry):
        r0 = pl.multiple_of(j * rt, rt)
        x0 = xr_ref[pl.ds(r0, rt), :].astype(f32)
        seg_first = jnp.logical_or(j == 0, j == n_ctx_tiles)
        seg_last = jnp.logical_or(j == n_ctx_tiles - 1, j == n_tiles - 1)
        rp = pl.multiple_of(jnp.maximum(r0 - BF16_ROWS, 0), BF16_ROWS)
        rn = pl.multiple_of(jnp.minimum(r0 + rt, s_all - BF16_ROWS), BF16_ROWS)
        prev = xr_ref[pl.ds(rp, BF16_ROWS), :].astype(f32)
        nxt = xr_ref[pl.ds(rn, BF16_ROWS), :].astype(f32)
        pm = jnp.where(seg_first, 0.0, 1.0)
        nm = jnp.where(seg_last, 0.0, 1.0)
        p1 = prev[BF16_ROWS - 1:BF16_ROWS] * pm
        p2 = prev[BF16_ROWS - 2:BF16_ROWS - 1] * pm
        n0 = nxt[0:1] * nm
        xm1 = jnp.where(row == 0, p1, pltpu.roll(x0, 1, 0))
        xm2 = jnp.where(row == 0, p2, jnp.where(row == 1, p1, pltpu.roll(x0, 2, 0)))
        xp1 = jnp.where(row == rt - 1, n0, pltpu.roll(x0, rt - 1, 0))
        xc_ref[pl.ds(r0, rt), :] = (cw[0:1] * xm2 + cw[1:2] * xm1 + cw[2:3] * x0 + cw[3:4] * xp1 + cbias)
        return carry

    lax.fori_loop(0, n_tiles, conv_tile, 0)

    def scan_dir(d, reverse):
        z = -lam_ref[d:d + 1, :]
        sp = jnp.maximum(z, 0.0) + jnp.log1p(jnp.exp(-jnp.abs(z)))
        w = wai_ref[d, 0]
        bias = bai_ref[d, 0]

        def tile_body(step, carry):
            if reverse:
                j = jnp.where(step < n_ctx_tiles, n_ctx_tiles - 1 - step, n_tiles - 1 - (step - n_ctx_tiles))
            else:
                j = step
            r0 = pl.multiple_of(j * rt, rt)
            xc = xc_ref[pl.ds(r0, rt), :]
            ri = jnp.dot(xc.astype(bf16), w, preferred_element_type=f32) + bias
            r = jax.nn.sigmoid(ri[:, :LRU_BLOCK])
            gi = jax.nn.sigmoid(ri[:, LRU_BLOCK:])
            a = jnp.exp(-LRU_C * r * sp)
            a_ref[...] = a
            b_ref[...] = jnp.sqrt(1.0 - a * a) * gi * xc

            def group(gidx, c):
                g = (rt // SUBLANES - 1 - gidx) if reverse else gidx
                o8 = pl.multiple_of(g * SUBLANES, SUBLANES)
                av = a_ref[pl.ds(o8, SUBLANES), :]
                bv = b_ref[pl.ds(o8, SUBLANES), :]
                for s in (1, 2, 4):
                    if reverse:
                        keep = row8 < SUBLANES - s
                        sh = SUBLANES - s
                    else:
                        keep = row8 >= s
                        sh = s
                    a_sh = jnp.where(keep, pltpu.roll(av, sh, 0), 1.0)
                    b_sh = jnp.where(keep, pltpu.roll(bv, sh, 0), 0.0)
                    bv = av * b_sh + bv
                    av = av * a_sh
                h = bv + av * c
                rows = pl.ds(pl.multiple_of(r0 + o8, SUBLANES), SUBLANES)
                if d == 0:
                    y_ref[rows, :] = h
                else:
                    y_ref[rows, :] = y_ref[rows, :] + h
                return h[0:1] if reverse else h[SUBLANES - 1:SUBLANES]

            return lax.fori_loop(0, rt // SUBLANES, group, carry)

        lax.fori_loop(0, n_tiles, tile_body, jnp.zeros((1, LANES), f32))

    scan_dir(0, False)
    scan_dir(1, True)

    def out_tile(j, carry):
        rows = pl.ds(pl.multiple_of(j * rt, rt), rt)
        o_ref[rows, :] = (y_ref[rows, :] * _gelu(gr_ref[rows, :].astype(f32))).astype(o_ref.dtype)
        return carry

    lax.fori_loop(0, n_tiles, out_tile, 0)


def _lru(proj, conv_w, conv_b, w_ai, b_ai, lam, batch, s_all, n_ctx, lw):
    t = proj.shape[0]
    ncb = lw // LRU_BLOCK
    kern = functools.partial(_lru_kernel, n_tiles=s_all // TOK_TILE, n_ctx_tiles=n_ctx // TOK_TILE)
    return pl.pallas_call(
        kern,
        grid=(batch, ncb),
        in_specs=[
            pl.BlockSpec((s_all, LRU_BLOCK), lambda b, c: (b, c)),
            pl.BlockSpec((s_all, LRU_BLOCK), lambda b, c: (b, ncb + c)),
            pl.BlockSpec((CONV_W, LRU_BLOCK), lambda b, c: (0, c)),
            pl.BlockSpec((1, LRU_BLOCK), lambda b, c: (0, c)),
            pl.BlockSpec((2, 1, LRU_BLOCK, 2 * LRU_BLOCK), lambda b, c: (0, c, 0, 0)),
            pl.BlockSpec((2, 1, 1, 2 * LRU_BLOCK), lambda b, c: (0, c, 0, 0)),
            pl.BlockSpec((2, LRU_BLOCK), lambda b, c: (0, c)),
        ],
        out_specs=pl.BlockSpec((s_all, LRU_BLOCK), lambda b, c: (b, c)),
        out_shape=jax.ShapeDtypeStruct((t, lw), bf16),
        scratch_shapes=[
            pltpu.VMEM((s_all, LRU_BLOCK), f32),
            pltpu.VMEM((s_all, LRU_BLOCK), f32),
            pltpu.VMEM((TOK_TILE, LRU_BLOCK), f32),
            pltpu.VMEM((TOK_TILE, LRU_BLOCK), f32),
        ],
        compiler_params=_cparams("arbitrary", "arbitrary"),
        name="rg_lru",
    )(proj, proj, conv_w, conv_b.reshape(1, lw), w_ai, b_ai, lam)


def _route_kernel(x_ref, g_ref, mod_ref, wr_ref, br_ref, tri_ref,
                  f_ref, idx_ref, gate_ref, rank_ref, cnt_ref, carry_ref, *, shift_row, scale_row):
    @pl.when(pl.program_id(0) == 0)
    def _():
        carry_ref[...] = jnp.zeros(carry_ref.shape, f32)

    m = mod_ref[0, 0]
    h = _rms_mod(x_ref[...], g_ref[...], m[shift_row:shift_row + 1], m[scale_row:scale_row + 1])
    f_ref[...] = _pack_halves(h)
    logits = jnp.dot(h.astype(bf16), wr_ref[...], preferred_element_type=f32) + br_ref[...]

    lane = lax.broadcasted_iota(i32, logits.shape, 1)
    vals = logits
    sels, tops, idxs = [], [], []
    for _ in range(TOP_K):
        mx = jnp.max(vals, axis=-1, keepdims=True)
        ik = jnp.min(jnp.where(vals == mx, lane, LANES), axis=-1, keepdims=True)
        sel = lane == ik
        sels.append(sel)
        tops.append(mx)
        idxs.append(ik)
        vals = jnp.where(sel, -jnp.inf, vals)
    es = [jnp.exp(tv - tops[0]) for tv in tops]
    denom = es[0] + es[1] + es[2] + es[3]

    onehot = jnp.zeros(logits.shape, f32)
    for sel in sels:
        onehot = onehot + jnp.where(sel, 1.0, 0.0)
    before = jnp.dot(tri_ref[...], onehot.astype(bf16), preferred_element_type=f32) + carry_ref[...]

    idx_o = jnp.zeros(logits.shape, i32)
    gate_o = jnp.zeros(logits.shape, f32)
    rank_o = jnp.zeros(logits.shape, i32)
    for k in range(TOP_K):
        rk = jnp.sum(jnp.where(sels[k], before, 0.0), axis=-1, keepdims=True).astype(i32)
        idx_o = jnp.where(lane == k, idxs[k], idx_o)
        gate_o = jnp.where(lane == k, es[k] / denom, gate_o)
        rank_o = jnp.where(lane == k, rk, rank_o)
    idx_ref[...] = idx_o
    gate_ref[...] = gate_o
    rank_ref[...] = rank_o
    carry = carry_ref[...] + jnp.sum(onehot, axis=0, keepdims=True)
    carry_ref[...] = carry
    cnt_ref[...] = jnp.broadcast_to(carry, cnt_ref.shape).astype(i32)


def _route(xs, g, mod, layer, mod_row, shift_row, scale_row, w_r, b_r):
    t, d = xs.shape
    n_exp = w_r.shape[1]
    wr = jnp.zeros((d, LANES), bf16).at[:, :n_exp].set(w_r.astype(bf16))
    br = jnp.full((1, LANES), -1e30, f32).at[0, :n_exp].set(b_r.astype(f32))
    r = jnp.arange(TOK_TILE)
    tri = (r[:, None] > r[None, :]).astype(bf16)
    tile = lambda w: pl.BlockSpec((TOK_TILE, w), lambda i: (i, 0))
    return pl.pallas_call(
        functools.partial(_route_kernel, shift_row=shift_row, scale_row=scale_row),
        grid=(t // TOK_TILE,),
        in_specs=[
            tile(d),
            pl.BlockSpec((1, d), lambda i: (0, 0)),
            pl.BlockSpec((1, 1, 6, d), lambda i: (layer, mod_row(i), 0, 0)),
            pl.BlockSpec((d, LANES), lambda i: (0, 0)),
            pl.BlockSpec((1, LANES), lambda i: (0, 0)),
            pl.BlockSpec((TOK_TILE, TOK_TILE), lambda i: (0, 0)),
        ],
        out_specs=[tile(d // 2), tile(LANES), tile(LANES), tile(LANES),
                   pl.BlockSpec((SUBLANES, LANES), lambda i: (0, 0))],
        out_shape=[
            jax.ShapeDtypeStruct((t, d // 2), u32),
            jax.ShapeDtypeStruct((t, LANES), i32),
            jax.ShapeDtypeStruct((t, LANES), f32),
            jax.ShapeDtypeStruct((t, LANES), i32),
            jax.ShapeDtypeStruct((SUBLANES, LANES), i32),
        ],
        scratch_shapes=[pltpu.VMEM((1, LANES), f32)],
        compiler_params=_cparams("arbitrary"),
        name="moe_route",
    )(xs, g.reshape(1, d), mod, wr, br, tri)


def _row_gather(idx_ref, src_hbm, buf, sem, n):
    def issue(j, carry):
        r = idx_ref[0, 0, j]
        pltpu.make_async_copy(src_hbm.at[pl.ds(r, 1), :], buf.at[pl.ds(j, 1), :], sem).start()
        return carry

    lax.fori_loop(0, n, issue, 0)
    pltpu.make_async_copy(src_hbm.at[pl.ds(0, n), :], buf, sem).wait()


def _expert_kernel(be_ref, nu_ref, tok_ref, f_hbm, w1_ref, b1_ref, w2_ref, b2_ref, y_ref, xbuf, sem):
    i = pl.program_id(0)

    @pl.when(i < nu_ref[0])
    def _():
        _row_gather(tok_ref, f_hbm, xbuf, sem, MOE_ROWS)
        lo, hi = _unpack_halves(xbuf[...])
        lo = lo.astype(bf16)
        hi = hi.astype(bf16)
        half = lo.shape[1]
        h = (jnp.dot(lo, w1_ref[0, :half, :], preferred_element_type=f32)
             + jnp.dot(hi, w1_ref[0, half:, :], preferred_element_type=f32) + b1_ref[0])
        lin = pltpu.roll(h, h.shape[1] - 1, 1)
        h_glu = jnp.minimum(h, SWIGLU_LIMIT)
        h_lin = jnp.clip(lin, -SWIGLU_LIMIT, SWIGLU_LIMIT)
        act = h_glu * jax.nn.sigmoid(SWIGLU_ALPHA * h_glu) * (h_lin + 1.0)
        y = jnp.dot(act.astype(bf16), w2_ref[0], preferred_element_type=f32) + b2_ref[0]
        y_ref[...] = _pack_halves(y)

    @pl.when(i >= nu_ref[0])
    def _():
        y_ref[...] = jnp.zeros(y_ref.shape, y_ref.dtype)


def _experts(block_expert, n_used, row_tok, fpk, w1, b1, w2, b2):
    n_blocks = block_expert.shape[0]
    t, half = fpk.shape
    n_exp, d, dff = w1.shape
    wspec = lambda shape: pl.BlockSpec((1,) + shape, lambda i, be, nu: (be[i], 0, 0))
    grid_spec = pltpu.PrefetchScalarGridSpec(
        num_scalar_prefetch=2,
        grid=(n_blocks,),
        in_specs=[
            pl.BlockSpec((1, 1, MOE_ROWS), lambda i, be, nu: (i, 0, 0), memory_space=pltpu.SMEM),
            pl.BlockSpec(memory_space=pl.ANY),
            wspec((d, dff)), wspec((1, dff)), wspec((dff, d)), wspec((1, d)),
        ],
        out_specs=pl.BlockSpec((MOE_ROWS, half), lambda i, be, nu: (i, 0)),
        scratch_shapes=[pltpu.VMEM((MOE_ROWS, half), u32), pltpu.SemaphoreType.DMA(())],
    )
    return pl.pallas_call(
        _expert_kernel,
        grid_spec=grid_spec,
        out_shape=jax.ShapeDtypeStruct((n_blocks * MOE_ROWS, half), u32),
        compiler_params=_cparams("arbitrary"),
        name="moe_experts",
    )(block_expert, n_used, row_tok.reshape(n_blocks, 1, MOE_ROWS), fpk, w1, b1, w2, b2)


def _combine_kernel(dest_ref, x_ref, gate_ref, mod_ref, y_hbm, o_ref, buf, sem, *, gate_row):
    _row_gather(dest_ref, y_hbm, buf, sem, TOP_K * TOK_TILE)
    half = buf.shape[1]
    acc_lo = jnp.zeros((TOK_TILE, half), f32)
    acc_hi = jnp.zeros((TOK_TILE, half), f32)
    for k in range(TOP_K):
        lo, hi = _unpack_halves(buf[k * TOK_TILE:(k + 1) * TOK_TILE, :])
        gk = gate_ref[:, k:k + 1]
        acc_lo = acc_lo + gk * lo
        acc_hi = acc_hi + gk * hi
    g2 = mod_ref[0, 0][gate_row:gate_row + 1]
    o_ref[:, :half] = x_ref[:, :half] + g2[:, :half] * acc_lo
    o_ref[:, half:] = x_ref[:, half:] + g2[:, half:] * acc_hi


def _combine(dest_tiles, xs, gates, mod, layer, mod_row, gate_row, ypk):
    t, d = xs.shape
    return pl.pallas_call(
        functools.partial(_combine_kernel, gate_row=gate_row),
        grid=(t // TOK_TILE,),
        in_specs=[
            pl.BlockSpec((1, 1, TOP_K * TOK_TILE), lambda i: (i, 0, 0), memory_space=pltpu.SMEM),
            pl.BlockSpec((TOK_TILE, d), lambda i: (i, 0)),
            pl.BlockSpec((TOK_TILE, LANES), lambda i: (i, 0)),
            pl.BlockSpec((1, 1, 6, d), lambda i: (layer, mod_row(i), 0, 0)),
            pl.BlockSpec(memory_space=pl.ANY),
        ],
        out_specs=pl.BlockSpec((TOK_TILE, d), lambda i: (i, 0)),
        out_shape=jax.ShapeDtypeStruct((t, d), f32),
        input_output_aliases={1: 0},
        scratch_shapes=[pltpu.VMEM((TOP_K * TOK_TILE, d // 2), u32), pltpu.SemaphoreType.DMA(())],
        compiler_params=_cparams("arbitrary"),
        name="moe_combine",
    )(dest_tiles, xs, gates, mod, ypk)


def _moe_layer(xs, g, mod, layer, mod_row, w_r, b_r, w1, b1, w2, b2):
    t, d = xs.shape
    n_exp = w_r.shape[1]
    fpk, idx_o, gate_o, rank_o, cnt = _route(xs, g, mod, layer, mod_row, 3, 4, w_r, b_r)
    counts = cnt[0, :n_exp]
    padded = (counts + MOE_ROWS - 1) // MOE_ROWS * MOE_ROWS
    pad_end = jnp.cumsum(padded)
    pad_start = pad_end - padded
    dest = pad_start[idx_o[:, :TOP_K]] + rank_o[:, :TOP_K]
    n_rows = t * TOP_K + n_exp * MOE_ROWS
    n_blocks = n_rows // MOE_ROWS
    tok = jnp.broadcast_to(jnp.arange(t, dtype=i32)[:, None], (t, TOP_K))
    row_tok = jnp.zeros((n_rows,), i32).at[dest.reshape(-1)].set(tok.reshape(-1))
    block_expert = jnp.minimum(
        jnp.searchsorted(pad_end, jnp.arange(n_blocks, dtype=i32) * MOE_ROWS, side="right"),
        n_exp - 1).astype(i32)
    n_used = (pad_end[-1:] // MOE_ROWS).astype(i32)
    ypk = _experts(block_expert, n_used, row_tok, fpk, w1, b1, w2, b2)
    nt = t // TOK_TILE
    dest_tiles = dest.reshape(nt, TOK_TILE, TOP_K).transpose(0, 2, 1).reshape(nt, 1, TOP_K * TOK_TILE)
    return _combine(dest_tiles.astype(i32), xs, gate_o, mod, layer, mod_row, 5, ypk)


def kernel(x, c, ctx, c_ctx, ada_w, ada_b, norm_mix_g, norm_ffn_g, ev_w_in, ev_q_g, ev_k_g, ev_lq1, ev_lk1, ev_lq2, ev_lk2, ev_subln_g, ev_sgu_g, ev_w_s, ev_b_s, ev_w_out, od_w_in, od_conv_w, od_conv_b, od_w_a, od_b_a, od_w_i, od_b_i, od_lam, od_w_out, moe_w_r, moe_b_r, moe_w1, moe_b1, moe_w2, moe_b2):
    batch, n_lat, d = x.shape
    n_ctx = ctx.shape[1]
    depth = ada_w.shape[0]
    s_all = n_ctx + n_lat
    assert n_ctx % TOK_TILE == 0 and n_lat % TOK_TILE == 0 and n_lat % GRID_W == 0
    assert n_ctx % KV_CTX_TILE == 0 and n_lat % KV_TILE == 0
    assert od_w_a.shape[-1] == LRU_BLOCK and ev_w_s.shape[-1] == CHUNK
    assert d % (2 * HEAD_DIM) == 0 and batch < SUBLANES
    tiles_per_sample = s_all // TOK_TILE
    ctx_tiles = n_ctx // TOK_TILE
    aw = d // 2
    sw = d - aw
    lw = od_w_out.shape[1]

    def mod_row(i):
        return jnp.where(i % tiles_per_sample < ctx_tiles, batch, i // tiles_per_sample)

    cvec = jnp.zeros((SUBLANES, d), f32).at[:batch].set(c).at[batch].set(c_ctx)
    mod = _ada_all(cvec, ada_w, ada_b).reshape(depth, SUBLANES, 6, d)
    rope = _rope_tables(n_lat, n_ctx)
    xs = jnp.concatenate([ctx, x], axis=1).reshape(batch * s_all, d)

    for l in range(depth):
        h = _norm_mod(xs, norm_mix_g[l], mod, l, mod_row, 0, 1)
        if l % 2 == 0:
            e = l // 2
            lam_init = 0.8 - 0.6 * math.exp(-0.3 * l)
            proj = _matmul(h, ev_w_in[e].astype(bf16))
            q, k = _qk_norm_rope(proj, ev_q_g[e], ev_k_g[e], rope, tiles_per_sample, aw)
            lam_params = jnp.stack([ev_lq1[e], ev_lk1[e], ev_lq2[e], ev_lk2[e]]).astype(f32)
            o = _diff_attention(q, k, proj, 2 * aw // HEAD_DIM, lam_params, ev_subln_g[e],
                                batch, s_all, n_ctx, lam_init)
            sg = _sgu(proj, 3 * aw // sw, ev_sgu_g[e], ev_w_s[e], ev_b_s[e], sw)
            xs = _outproj_residual([o, sg], ev_w_out[e].astype(bf16), xs, mod, l, mod_row, 2)
        else:
            o_ = l // 2
            proj = _matmul(h, od_w_in[o_].astype(bf16))
            w_ai = jnp.concatenate([od_w_a[o_], od_w_i[o_]], axis=-1).astype(bf16)
            b_ai = jnp.concatenate([od_b_a[o_], od_b_i[o_]], axis=-1)[:, :, None, :].astype(f32)
            y = _lru(proj, od_conv_w[o_], od_conv_b[o_], w_ai, b_ai, od_lam[o_], batch, s_all, n_ctx, lw)
            xs = _outproj_residual([y], od_w_out[o_].astype(bf16), xs, mod, l, mod_row, 2)
        w2 = moe_w2[l].astype(bf16)
        w2x = jnp.stack([w2, jnp.zeros_like(w2)], axis=2).reshape(w2.shape[0], 2 * w2.shape[1], w2.shape[2])
        xs = _moe_layer(
            xs, norm_ffn_g[l], mod, l, mod_row, moe_w_r[l], moe_b_r[l],
            moe_w1[l].astype(bf16), moe_b1[l][:, None, :], w2x, moe_b2[l][:, None, :])
    return xs.reshape(batch, s_all, d)[:, n_ctx:, :]
```

```python
import functools
import math

import jax
import jax.numpy as jnp
from jax import lax
from jax.experimental import pallas as pl
from jax.experimental.pallas import tpu as pltpu

f32 = jnp.float32
bf16 = jnp.bfloat16
u32 = jnp.uint32
i32 = jnp.int32

EPS = 1e-6
GRID_W = 64
HEAD_DIM = 128
QK_DIM = 64
ROPE_BASE = 10000.0
CHUNK = 128
LRU_BLOCK = 128
CONV_W = 4
LRU_C = 8.0
TOP_K = 4
SWIGLU_LIMIT = 7.0
SWIGLU_ALPHA = 1.702

LANES = 128
SUBLANES = 8
BF16_ROWS = 16
VMEM_LIMIT = 56 * 1024 * 1024

TOK_TILE = 256
MOE_ROWS = 256
KV_TILE = 1024
KV_CTX_TILE = 256


def _cparams(*sem):
    return pltpu.CompilerParams(dimension_semantics=sem, vmem_limit_bytes=VMEM_LIMIT)


def _pick(n, options):
    for o in options:
        if n % o == 0:
            return o
    raise ValueError(f"no tile for {n} in {options}")


def _gelu(x):
    return 0.5 * x * (1.0 + jnp.tanh(math.sqrt(2.0 / math.pi) * (x + 0.044715 * (x * x * x))))


def _rms_mod(x, g, shift, scale):
    y = x * lax.rsqrt(jnp.mean(x * x, axis=-1, keepdims=True) + EPS) * g
    return y * (1.0 + scale) + shift


def _pack_halves(y):
    half = y.shape[1] // 2
    yb = y.astype(bf16).astype(f32)
    lo = lax.shift_right_logical(lax.bitcast_convert_type(yb[:, :half], u32), jnp.uint32(16))
    hi = lax.bitcast_convert_type(yb[:, half:], u32) & jnp.uint32(0xFFFF0000)
    return lo | hi


def _unpack_halves(u):
    lo = lax.bitcast_convert_type(lax.shift_left(u, jnp.uint32(16)), f32)
    hi = lax.bitcast_convert_type(u & jnp.uint32(0xFFFF0000), f32)
    return lo, hi


def _ada_kernel(c_ref, w_ref, b_ref, o_ref):
    cv = c_ref[...]
    s = (cv * jax.nn.sigmoid(cv)).astype(bf16)
    o_ref[0] = jnp.dot(s, w_ref[0].astype(bf16), preferred_element_type=f32) + b_ref[0]


def _ada_all(cvec, ada_w, ada_b):
    depth, d, n6 = ada_w.shape
    rows = cvec.shape[0]
    tn = _pick(n6, (1024, 512, 256, 128))
    return pl.pallas_call(
        _ada_kernel,
        grid=(depth, n6 // tn),
        in_specs=[
            pl.BlockSpec((rows, d), lambda l, j: (0, 0)),
            pl.BlockSpec((1, d, tn), lambda l, j: (l, 0, j)),
            pl.BlockSpec((1, 1, tn), lambda l, j: (l, 0, j)),
        ],
        out_specs=pl.BlockSpec((1, rows, tn), lambda l, j: (l, 0, j)),
        out_shape=jax.ShapeDtypeStruct((depth, rows, n6), f32),
        compiler_params=_cparams("arbitrary", "arbitrary"),
        name="ada_mod",
    )(cvec, ada_w, ada_b.reshape(depth, 1, n6))


def _norm_mod_kernel(x_ref, g_ref, mod_ref, h_ref, *, shift_row, scale_row):
    m = mod_ref[0, 0]
    h = _rms_mod(x_ref[...], g_ref[...], m[shift_row:shift_row + 1], m[scale_row:scale_row + 1])
    h_ref[...] = h.astype(h_ref.dtype)


def _norm_mod(xs, g, mod, layer, mod_row, shift_row, scale_row):
    t, d = xs.shape
    return pl.pallas_call(
        functools.partial(_norm_mod_kernel, shift_row=shift_row, scale_row=scale_row),
        grid=(t // TOK_TILE,),
        in_specs=[
            pl.BlockSpec((TOK_TILE, d), lambda i: (i, 0)),
            pl.BlockSpec((1, d), lambda i: (0, 0)),
            pl.BlockSpec((1, 1, 6, d), lambda i: (layer, mod_row(i), 0, 0)),
        ],
        out_specs=pl.BlockSpec((TOK_TILE, d), lambda i: (i, 0)),
        out_shape=jax.ShapeDtypeStruct((t, d), bf16),
        compiler_params=_cparams("arbitrary"),
        name="norm_mod",
    )(xs, g.reshape(1, d), mod)


def _matmul_kernel(a_ref, w_ref, o_ref):
    o_ref[...] = jnp.dot(a_ref[...], w_ref[...], preferred_element_type=f32).astype(o_ref.dtype)


def _matmul(a, w, out_dtype=bf16):
    m, k = a.shape
    n = w.shape[1]
    tm = _pick(m, (1024, 768, 512, 256))
    tn = _pick(n, (1024, 512, 256, 128))
    return pl.pallas_call(
        _matmul_kernel,
        grid=(m // tm, n // tn),
        in_specs=[
            pl.BlockSpec((tm, k), lambda i, j: (i, 0)),
            pl.BlockSpec((k, tn), lambda i, j: (0, j)),
        ],
        out_specs=pl.BlockSpec((tm, tn), lambda i, j: (i, j)),
        out_shape=jax.ShapeDtypeStruct((m, n), out_dtype),
        compiler_params=_cparams("arbitrary", "arbitrary"),
        name="in_proj",
    )(a, w)


def _outproj_kernel(*refs, n_a, gate_row):
    a_refs = refs[:n_a]
    w_ref, x_ref, mod_ref, o_ref = refs[n_a:]
    acc = None
    off = 0
    for a in a_refs:
        ka = a.shape[1]
        part = jnp.dot(a[...], w_ref[off:off + ka, :], preferred_element_type=f32)
        acc = part if acc is None else acc + part
        off += ka
    gate = mod_ref[0, 0][gate_row:gate_row + 1]
    o_ref[...] = x_ref[...] + gate * acc


def _outproj_residual(a_list, w, xs, mod, layer, mod_row, gate_row):
    t, d = xs.shape
    kw = w.shape[0]
    n_a = len(a_list)
    in_specs = [pl.BlockSpec((TOK_TILE, a.shape[1]), lambda i: (i, 0)) for a in a_list]
    in_specs += [
        pl.BlockSpec((kw, d), lambda i: (0, 0)),
        pl.BlockSpec((TOK_TILE, d), lambda i: (i, 0)),
        pl.BlockSpec((1, 1, 6, d), lambda i: (layer, mod_row(i), 0, 0)),
    ]
    return pl.pallas_call(
        functools.partial(_outproj_kernel, n_a=n_a, gate_row=gate_row),
        grid=(t // TOK_TILE,),
        in_specs=in_specs,
        out_specs=pl.BlockSpec((TOK_TILE, d), lambda i: (i, 0)),
        out_shape=jax.ShapeDtypeStruct((t, d), f32),
        input_output_aliases={n_a + 1: 0},
        compiler_params=_cparams("arbitrary"),
        name="out_proj",
    )(*a_list, w, xs, mod)


def _qk_kernel(q_ref, k_ref, qg_ref, kg_ref, cos_ref, sa_ref, sb_ref, qo_ref, ko_ref, *, heads):
    lane = lax.broadcasted_iota(i32, (TOK_TILE, LANES), 1)
    first = lane < QK_DIM
    cos_t = cos_ref[...]
    sin_a = sa_ref[...]
    sin_b = sb_ref[...]

    def one(src, g_ref, dst, scale):
        g = g_ref[...]
        for h in range(heads):
            cols = slice(h * HEAD_DIM, (h + 1) * HEAD_DIM)
            x = src[:, cols].astype(f32)
            sq = x * x
            s0 = jnp.sum(jnp.where(first, sq, 0.0), axis=-1, keepdims=True)
            s1 = jnp.sum(jnp.where(first, 0.0, sq), axis=-1, keepdims=True)
            ms = jnp.where(first, s0, s1) * (1.0 / QK_DIM)
            y = x * lax.rsqrt(ms + EPS) * g
            y = y * cos_t + pltpu.roll(y, LANES - 16, 1) * sin_a + pltpu.roll(y, 16, 1) * sin_b
            dst[:, cols] = (y * scale).astype(dst.dtype)

    one(q_ref, qg_ref, qo_ref, QK_DIM ** -0.5)
    one(k_ref, kg_ref, ko_ref, 1.0)


def _qk_norm_rope(proj, q_g, k_g, rope, tiles_per_sample, aw):
    t = proj.shape[0]
    heads = aw // HEAD_DIM
    cos_t, sin_a, sin_b = rope
    g2 = lambda g: jnp.concatenate([g, g]).reshape(1, LANES).astype(f32)
    tab = pl.BlockSpec((TOK_TILE, LANES), lambda i: (i % tiles_per_sample, 0))
    return pl.pallas_call(
        functools.partial(_qk_kernel, heads=heads),
        grid=(t // TOK_TILE,),
        in_specs=[
            pl.BlockSpec((TOK_TILE, aw), lambda i: (i, 0)),
            pl.BlockSpec((TOK_TILE, aw), lambda i: (i, 1)),
            pl.BlockSpec((1, LANES), lambda i: (0, 0)),
            pl.BlockSpec((1, LANES), lambda i: (0, 0)),
            tab, tab, tab,
        ],
        out_specs=[pl.BlockSpec((TOK_TILE, aw), lambda i: (i, 0))] * 2,
        out_shape=[jax.ShapeDtypeStruct((t, aw), bf16)] * 2,
        compiler_params=_cparams("arbitrary"),
        name="qk_norm_rope",
    )(proj, proj, g2(q_g), g2(k_g), cos_t, sin_a, sin_b)


def _rope_tables(n_lat, n_ctx):
    tpos = jnp.arange(n_lat)
    row = (tpos // GRID_W).astype(f32)
    col = (tpos % GRID_W).astype(f32)
    axis_dim = QK_DIM // 2
    nfreq = axis_dim // 2
    inv_freq = ROPE_BASE ** (-jnp.arange(0, axis_dim, 2, dtype=f32) / axis_dim)
    lane = jnp.arange(LANES)
    is_col = ((lane % QK_DIM) // axis_dim) == 1
    second = ((lane % axis_dim) // nfreq) == 1
    fr = inv_freq[lane % nfreq]
    pos = jnp.where(is_col[None, :], col[:, None], row[:, None])
    ang = pos * fr[None, :]
    cos_t = jnp.cos(ang)
    sin = jnp.sin(ang)
    sin_a = jnp.where(second[None, :], 0.0, -sin)
    sin_b = jnp.where(second[None, :], sin, 0.0)
    pad = lambda a, v: jnp.concatenate([jnp.full((n_ctx, LANES), v, f32), a], axis=0)
    return pad(cos_t, 1.0), pad(sin_a, 0.0), pad(sin_b, 0.0)


def _attn_kernel(q_ref, k_ref, v_ref, lp_ref, sg_ref, o_ref, qs_ref, m_ref, l_ref, acc_ref,
                 *, n_ctx_q, ctx_len, seq_all, lam_init):
    tq = TOK_TILE
    qi = pl.program_id(2)
    lane = lax.broadcasted_iota(i32, (tq, LANES), 1)
    q = q_ref[...]
    zero = jnp.zeros_like(q)
    qs_ref[0:tq, :] = jnp.where(lane < QK_DIM, q, zero)
    qs_ref[tq:2 * tq, :] = jnp.where(lane < QK_DIM, zero, q)
    m_ref[...] = jnp.full(m_ref.shape, -jnp.inf, f32)
    l_ref[...] = jnp.zeros(l_ref.shape, f32)
    acc_ref[...] = jnp.zeros(acc_ref.shape, f32)
    def kv_step(off, tk):
        kt = k_ref[pl.ds(off, tk), :]
        vt = v_ref[pl.ds(off, tk), :]
        s = lax.dot_general(qs_ref[...], kt, (((1,), (1,)), ((), ())), preferred_element_type=f32)
        m_prev = m_ref[...]
        m_new = jnp.maximum(m_prev, jnp.max(s, axis=-1, keepdims=True))
        alpha = jnp.exp(m_prev - m_new)
        p = jnp.exp(s - m_new)
        l_ref[...] = alpha * l_ref[...] + jnp.sum(p, axis=-1, keepdims=True)
        acc_ref[...] = alpha * acc_ref[...] + jnp.dot(p.astype(bf16), vt, preferred_element_type=f32)
        m_ref[...] = m_new

    def ctx_body(j, carry):
        kv_step(pl.multiple_of(j * KV_CTX_TILE, KV_CTX_TILE), KV_CTX_TILE)
        return carry

    lax.fori_loop(0, ctx_len // KV_CTX_TILE, ctx_body, 0)

    def lat_body(j, carry):
        kv_step(pl.multiple_of(ctx_len + j * KV_TILE, KV_CTX_TILE), KV_TILE)
        return carry

    lax.fori_loop(0, jnp.where(qi < n_ctx_q, 0, (seq_all - ctx_len) // KV_TILE), lat_body, 0)

    lp = lp_ref[...]
    lam = (jnp.exp(jnp.sum(lp[0:1] * lp[1:2], axis=-1, keepdims=True))
           - jnp.exp(jnp.sum(lp[2:3] * lp[3:4], axis=-1, keepdims=True)) + lam_init)
    o1 = acc_ref[0:tq, :] / l_ref[0:tq, :]
    o2 = acc_ref[tq:2 * tq, :] / l_ref[tq:2 * tq, :]
    o = o1 - lam * o2
    y = o * lax.rsqrt(jnp.mean(o * o, axis=-1, keepdims=True) + EPS) * sg_ref[...]
    o_ref[...] = (y * (1.0 - lam_init)).astype(o_ref.dtype)


def _diff_attention(q, k, v_src, v_col0, lam_params, subln_g, batch, s_all, n_ctx, lam_init):
    t, aw = q.shape
    heads = aw // HEAD_DIM
    nq = s_all // TOK_TILE
    kern = functools.partial(_attn_kernel, n_ctx_q=n_ctx // TOK_TILE, ctx_len=n_ctx, seq_all=s_all,
                             lam_init=lam_init)
    return pl.pallas_call(
        kern,
        grid=(batch, heads, nq),
        in_specs=[
            pl.BlockSpec((TOK_TILE, HEAD_DIM), lambda b, h, i: (b * nq + i, h)),
            pl.BlockSpec((s_all, HEAD_DIM), lambda b, h, i: (b, h)),
            pl.BlockSpec((s_all, HEAD_DIM), lambda b, h, i: (b, v_col0 + h)),
            pl.BlockSpec((4, QK_DIM), lambda b, h, i: (0, 0)),
            pl.BlockSpec((1, HEAD_DIM), lambda b, h, i: (0, 0)),
        ],
        out_specs=pl.BlockSpec((TOK_TILE, HEAD_DIM), lambda b, h, i: (b * nq + i, h)),
        out_shape=jax.ShapeDtypeStruct((t, aw), bf16),
        scratch_shapes=[
            pltpu.VMEM((2 * TOK_TILE, HEAD_DIM), bf16),
            pltpu.VMEM((2 * TOK_TILE, 1), f32),
            pltpu.VMEM((2 * TOK_TILE, 1), f32),
            pltpu.VMEM((2 * TOK_TILE, HEAD_DIM), f32),
        ],
        compiler_params=_cparams("arbitrary", "arbitrary", "arbitrary"),
        name="diff_attention",
    )(q, k, v_src, lam_params, subln_g.reshape(1, HEAD_DIM))


def _sgu_kernel(u_ref, v_ref, g_ref, ws_ref, bs_ref, o_ref, *, groups):
    v = _gelu(v_ref[...].astype(f32))
    vn = v * lax.rsqrt(jnp.mean(v * v, axis=-1, keepdims=True) + EPS) * g_ref[...]
    vb = vn.astype(bf16)
    for g in range(groups):
        cols = slice(g * CHUNK, (g + 1) * CHUNK)
        bias = bs_ref[:, g:g + 1]
        for c in range(TOK_TILE // CHUNK):
            rows = slice(c * CHUNK, (c + 1) * CHUNK)
            mixed = jnp.dot(ws_ref[g], vb[rows, cols], preferred_element_type=f32) + bias
            u = _gelu(u_ref[rows, cols].astype(f32))
            o_ref[rows, cols] = (u * mixed).astype(o_ref.dtype)


def _sgu(proj, u_blk, sgu_g, w_s, b_s, sw):
    t = proj.shape[0]
    groups = sw // CHUNK
    return pl.pallas_call(
        functools.partial(_sgu_kernel, groups=groups),
        grid=(t // TOK_TILE,),
        in_specs=[
            pl.BlockSpec((TOK_TILE, sw), lambda i: (i, u_blk)),
            pl.BlockSpec((TOK_TILE, sw), lambda i: (i, u_blk + 1)),
            pl.BlockSpec((1, sw), lambda i: (0, 0)),
            pl.BlockSpec((groups, CHUNK, CHUNK), lambda i: (0, 0, 0)),
            pl.BlockSpec((CHUNK, groups), lambda i: (0, 0)),
        ],
        out_specs=pl.BlockSpec((TOK_TILE, sw), lambda i: (i, 0)),
        out_shape=jax.ShapeDtypeStruct((t, sw), bf16),
        compiler_params=_cparams("arbitrary"),
        name="spatial_gating",
    )(proj, proj, sgu_g.reshape(1, sw), w_s.astype(bf16), b_s.T.astype(f32))


def _lru_kernel(xr_ref, gr_ref, cw_ref, cb_ref, wai_ref, bai_ref, lam_ref, o_ref,
                xc_ref, y_ref, a_ref, b_ref, *, n_tiles, n_ctx_tiles):
    rt = TOK_TILE
    s_all = n_tiles * rt
    row = lax.broadcasted_iota(i32, (rt, LANES), 0)
    row8 = lax.broadcasted_iota(i32, (SUBLANES, LANES), 0)
    cw = cw_ref[...]
    cbias = cb_ref[...]

    def conv_tile(j, carry):
        r0 = pl.multiple_of(j * rt, rt)
        x0 = xr_ref[pl.ds(r0, rt), :].astype(f32)
        seg_first = jnp.logical_or(j == 0, j == n_ctx_tiles)
        seg_last = jnp.logical_or(j == n_ctx_tiles - 1, j == n_tiles - 1)
        rp = pl.multiple_of(jnp.maximum(r0 - BF16_ROWS, 0), BF16_ROWS)
        rn = pl.multiple_of(jnp.minimum(r0 + rt, s_all - BF16_ROWS), BF16_ROWS)
        prev = xr_ref[pl.ds(rp, BF16_ROWS), :].astype(f32)
        nxt = xr_ref[pl.ds(rn, BF16_ROWS), :].astype(f32)
        pm = jnp.where(seg_first, 0.0, 1.0)
        nm = jnp.where(seg_last, 0.0, 1.0)
        p1 = prev[BF16_ROWS - 1:BF16_ROWS] * pm
        p2 = prev[BF16_ROWS - 2:BF16_ROWS - 1] * pm
        n0 = nxt[0:1] * nm
        xm1 = jnp.where(row == 0, p1, pltpu.roll(x0, 1, 0))
        xm2 = jnp.where(row == 0, p2, jnp.where(row == 1, p1, pltpu.roll(x0, 2, 0)))
        xp1 = jnp.where(row == rt - 1, n0, pltpu.roll(x0, rt - 1, 0))
        xc_ref[pl.ds(r0, rt), :] = (cw[0:1] * xm2 + cw[1:2] * xm1 + cw[2:3] * x0 + cw[3:4] * xp1 + cbias)
        return carry

    lax.fori_loop(0, n_tiles, conv_tile, 0)

    def scan_dir(d, reverse):
        z = -lam_ref[d:d + 1, :]
        sp = jnp.maximum(z, 0.0) + jnp.log1p(jnp.exp(-jnp.abs(z)))
        w = wai_ref[d, 0]
        bias = bai_ref[d, 0]

        def tile_body(step, carry):
            if reverse:
                j = jnp.where(step < n_ctx_tiles, n_ctx_tiles - 1 - step, n_tiles - 1 - (step - n_ctx_tiles))
            else:
                j = step
            r0 = pl.multiple_of(j * rt, rt)
            xc = xc_ref[pl.ds(r0, rt), :]
            ri = jnp.dot(xc.astype(bf16), w, preferred_element_type=f32) + bias
            r = jax.nn.sigmoid(ri[:, :LRU_BLOCK])
            gi = jax.nn.sigmoid(ri[:, LRU_BLOCK:])
            a = jnp.exp(-LRU_C * r * sp)
            a_ref[...] = a
            b_ref[...] = jnp.sqrt(1.0 - a * a) * gi * xc

            def group(gidx, c):
                g = (rt // SUBLANES - 1 - gidx) if reverse else gidx
                o8 = pl.multiple_of(g * SUBLANES, SUBLANES)
                av = a_ref[pl.ds(o8, SUBLANES), :]
                bv = b_ref[pl.ds(o8, SUBLANES), :]
                for s in (1, 2, 4):
                    if reverse:
                        keep = row8 < SUBLANES - s
                        sh = SUBLANES - s
                    else:
                        keep = row8 >= s
                        sh = s
                    a_sh = jnp.where(keep, pltpu.roll(av, sh, 0), 1.0)
                    b_sh = jnp.where(keep, pltpu.roll(bv, sh, 0), 0.0)
                    bv = av * b_sh + bv
                    av = av * a_sh
                h = bv + av * c
                rows = pl.ds(pl.multiple_of(r0 + o8, SUBLANES), SUBLANES)
                if d == 0:
                    y_ref[rows, :] = h
                else:
                    y_ref[rows, :] = y_ref[rows, :] + h
                return h[0:1] if reverse else h[SUBLANES - 1:SUBLANES]

            return lax.fori_loop(0, rt // SUBLANES, group, carry)

        lax.fori_loop(0, n_tiles, tile_body, jnp.zeros((1, LANES), f32))

    scan_dir(0, False)
    scan_dir(1, True)

    def out_tile(j, carry):
        rows = pl.ds(pl.multiple_of(j * rt, rt), rt)
        o_ref[rows, :] = (y_ref[rows, :] * _gelu(gr_ref[rows, :].astype(f32))).astype(o_ref.dtype)
        return carry

    lax.fori_loop(0, n_tiles, out_tile, 0)


def _lru(proj, conv_w, conv_b, w_ai, b_ai, lam, batch, s_all, n_ctx, lw):
    t = proj.shape[0]
    ncb = lw // LRU_BLOCK
    kern = functools.partial(_lru_kernel, n_tiles=s_all // TOK_TILE, n_ctx_tiles=n_ctx // TOK_TILE)
    return pl.pallas_call(
        kern,
        grid=(batch, ncb),
        in_specs=[
            pl.BlockSpec((s_all, LRU_BLOCK), lambda b, c: (b, c)),
            pl.BlockSpec((s_all, LRU_BLOCK), lambda b, c: (b, ncb + c)),
            pl.BlockSpec((CONV_W, LRU_BLOCK), lambda b, c: (0, c)),
            pl.BlockSpec((1, LRU_BLOCK), lambda b, c: (0, c)),
            pl.BlockSpec((2, 1, LRU_BLOCK, 2 * LRU_BLOCK), lambda b, c: (0, c, 0, 0)),
            pl.BlockSpec((2, 1, 1, 2 * LRU_BLOCK), lambda b, c: (0, c, 0, 0)),
            pl.BlockSpec((2, LRU_BLOCK), lambda b, c: (0, c)),
        ],
        out_specs=pl.BlockSpec((s_all, LRU_BLOCK), lambda b, c: (b, c)),
        out_shape=jax.ShapeDtypeStruct((t, lw), bf16),
        scratch_shapes=[
            pltpu.VMEM((s_all, LRU_BLOCK), f32),
            pltpu.VMEM((s_all, LRU_BLOCK), f32),
            pltpu.VMEM((TOK_TILE, LRU_BLOCK), f32),
            pltpu.VMEM((TOK_TILE, LRU_BLOCK), f32),
        ],
        compiler_params=_cparams("arbitrary", "arbitrary"),
        name="rg_lru",
    )(proj, proj, conv_w, conv_b.reshape(1, lw), w_ai, b_ai, lam)


def _route_kernel(x_ref, g_ref, mod_ref, wr_ref, br_ref, tri_ref,
                  f_ref, idx_ref, gate_ref, rank_ref, cnt_ref, carry_ref, *, shift_row, scale_row):
    @pl.when(pl.program_id(0) == 0)
    def _():
        carry_ref[...] = jnp.zeros(carry_ref.shape, f32)

    m = mod_ref[0, 0]
    h = _rms_mod(x_ref[...], g_ref[...], m[shift_row:shift_row + 1], m[scale_row:scale_row + 1])
    f_ref[...] = _pack_halves(h)
    logits = jnp.dot(h.astype(bf16), wr_ref[...], preferred_element_type=f32) + br_ref[...]

    lane = lax.broadcasted_iota(i32, logits.shape, 1)
    vals = logits
    sels, tops, idxs = [], [], []
    for _ in range(TOP_K):
        mx = jnp.max(vals, axis=-1, keepdims=True)
        ik = jnp.min(jnp.where(vals == mx, lane, LANES), axis=-1, keepdims=True)
        sel = lane == ik
        sels.append(sel)
        tops.append(mx)
        idxs.append(ik)
        vals = jnp.where(sel, -jnp.inf, vals)
    es = [jnp.exp(tv - tops[0]) for tv in tops]
    denom = es[0] + es[1] + es[2] + es[3]

    onehot = jnp.zeros(logits.shape, f32)
    for sel in sels:
        onehot = onehot + jnp.where(sel, 1.0, 0.0)
    before = jnp.dot(tri_ref[...], onehot.astype(bf16), preferred_element_type=f32) + carry_ref[...]

    idx_o = jnp.zeros(logits.shape, i32)
    gate_o = jnp.zeros(logits.shape, f32)
    rank_o = jnp.zeros(logits.shape, i32)
    for k in range(TOP_K):
        rk = jnp.sum(jnp.where(sels[k], before, 0.0), axis=-1, keepdims=True).astype(i32)
        idx_o = jnp.where(lane == k, idxs[k], idx_o)
        gate_o = jnp.where(lane == k, es[k] / denom, gate_o)
        rank_o = jnp.where(lane == k, rk, rank_o)
    idx_ref[...] = idx_o
    gate_ref[...] = gate_o
    rank_ref[...] = rank_o
    carry = carry_ref[...] + jnp.sum(onehot, axis=0, keepdims=True)
    carry_ref[...] = carry
    cnt_ref[...] = jnp.broadcast_to(carry, cnt_ref.shape).astype(i32)


def _route(xs, g, mod, layer, mod_row, shift_row, scale_row, w_r, b_r):
    t, d = xs.shape
    n_exp = w_r.shape[1]
    wr = jnp.zeros((d, LANES), bf16).at[:, :n_exp].set(w_r.astype(bf16))
    br = jnp.full((1, LANES), -1e30, f32).at[0, :n_exp].set(b_r.astype(f32))
    r = jnp.arange(TOK_TILE)
    tri = (r[:, None] > r[None, :]).astype(bf16)
    tile = lambda w: pl.BlockSpec((TOK_TILE, w), lambda i: (i, 0))
    return pl.pallas_call(
        functools.partial(_route_kernel, shift_row=shift_row, scale_row=scale_row),
        grid=(t // TOK_TILE,),
        in_specs=[
            tile(d),
            pl.BlockSpec((1, d), lambda i: (0, 0)),
            pl.BlockSpec((1, 1, 6, d), lambda i: (layer, mod_row(i), 0, 0)),
            pl.BlockSpec((d, LANES), lambda i: (0, 0)),
            pl.BlockSpec((1, LANES), lambda i: (0, 0)),
            pl.BlockSpec((TOK_TILE, TOK_TILE), lambda i: (0, 0)),
        ],
        out_specs=[tile(d // 2), tile(LANES), tile(LANES), tile(LANES),
                   pl.BlockSpec((SUBLANES, LANES), lambda i: (0, 0))],
        out_shape=[
            jax.ShapeDtypeStruct((t, d // 2), u32),
            jax.ShapeDtypeStruct((t, LANES), i32),
            jax.ShapeDtypeStruct((t, LANES), f32),
            jax.ShapeDtypeStruct((t, LANES), i32),
            jax.ShapeDtypeStruct((SUBLANES, LANES), i32),
        ],
        scratch_shapes=[pltpu.VMEM((1, LANES), f32)],
        compiler_params=_cparams("arbitrary"),
        name="moe_route",
    )(xs, g.reshape(1, d), mod, wr, br, tri)


def _row_gather(idx_ref, src_hbm, buf, sem, n):
    def issue(j, carry):
        r = idx_ref[0, 0, j]
        pltpu.make_async_copy(src_hbm.at[pl.ds(r, 1), :], buf.at[pl.ds(j, 1), :], sem).start()
        return carry

    lax.fori_loop(0, n, issue, 0)
    pltpu.make_async_copy(src_hbm.at[pl.ds(0, n), :], buf, sem).wait()


def _expert_kernel(be_ref, nu_ref, tok_ref, f_hbm, w1_ref, b1_ref, w2_ref, b2_ref, y_ref, xbuf, sem):
    i = pl.program_id(0)

    @pl.when(i < nu_ref[0])
    def _():
        _row_gather(tok_ref, f_hbm, xbuf, sem, MOE_ROWS)
        lo, hi = _unpack_halves(xbuf[...])
        lo = lo.astype(bf16)
        hi = hi.astype(bf16)
        half = lo.shape[1]
        h = (jnp.dot(lo, w1_ref[0, :half, :], preferred_element_type=f32)
             + jnp.dot(hi, w1_ref[0, half:, :], preferred_element_type=f32) + b1_ref[0])
        lin = pltpu.roll(h, h.shape[1] - 1, 1)
        h_glu = jnp.minimum(h, SWIGLU_LIMIT)
        h_lin = jnp.clip(lin, -SWIGLU_LIMIT, SWIGLU_LIMIT)
        act = h_glu * jax.nn.sigmoid(SWIGLU_ALPHA * h_glu) * (h_lin + 1.0)
        y = jnp.dot(act.astype(bf16), w2_ref[0], preferred_element_type=f32) + b2_ref[0]
        y_ref[...] = _pack_halves(y)

    @pl.when(i >= nu_ref[0])
    def _():
        y_ref[...] = jnp.zeros(y_ref.shape, y_ref.dtype)


def _experts(block_expert, n_used, row_tok, fpk, w1, b1, w2, b2):
    n_blocks = block_expert.shape[0]
    t, half = fpk.shape
    n_exp, d, dff = w1.shape
    wspec = lambda shape: pl.BlockSpec((1,) + shape, lambda i, be, nu: (be[i], 0, 0))
    grid_spec = pltpu.PrefetchScalarGridSpec(
        num_scalar_prefetch=2,
        grid=(n_blocks,),
        in_specs=[
            pl.BlockSpec((1, 1, MOE_ROWS), lambda i, be, nu: (i, 0, 0), memory_space=pltpu.SMEM),
            pl.BlockSpec(memory_space=pl.ANY),
            wspec((d, dff)), wspec((1, dff)), wspec((dff, d)), wspec((1, d)),
        ],
        out_specs=pl.BlockSpec((MOE_ROWS, half), lambda i, be, nu: (i, 0)),
        scratch_shapes=[pltpu.VMEM((MOE_ROWS, half), u32), pltpu.SemaphoreType.DMA(())],
    )
    return pl.pallas_call(
        _expert_kernel,
        grid_spec=grid_spec,
        out_shape=jax.ShapeDtypeStruct((n_blocks * MOE_ROWS, half), u32),
        compiler_params=_cparams("arbitrary"),
        name="moe_experts",
    )(block_expert, n_used, row_tok.reshape(n_blocks, 1, MOE_ROWS), fpk, w1, b1, w2, b2)


def _combine_kernel(dest_ref, x_ref, gate_ref, mod_ref, y_hbm, o_ref, buf, sem, *, gate_row):
    _row_gather(dest_ref, y_hbm, buf, sem, TOP_K * TOK_TILE)
    half = buf.shape[1]
    acc_lo = jnp.zeros((TOK_TILE, half), f32)
    acc_hi = jnp.zeros((TOK_TILE, half), f32)
    for k in range(TOP_K):
        lo, hi = _unpack_halves(buf[k * TOK_TILE:(k + 1) * TOK_TILE, :])
        gk = gate_ref[:, k:k + 1]
        acc_lo = acc_lo + gk * lo
        acc_hi = acc_hi + gk * hi
    g2 = mod_ref[0, 0][gate_row:gate_row + 1]
    o_ref[:, :half] = x_ref[:, :half] + g2[:, :half] * acc_lo
    o_ref[:, half:] = x_ref[:, half:] + g2[:, half:] * acc_hi


def _combine(dest_tiles, xs, gates, mod, layer, mod_row, gate_row, ypk):
    t, d = xs.shape
    return pl.pallas_call(
        functools.partial(_combine_kernel, gate_row=gate_row),
        grid=(t // TOK_TILE,),
        in_specs=[
            pl.BlockSpec((1, 1, TOP_K * TOK_TILE), lambda i: (i, 0, 0), memory_space=pltpu.SMEM),
            pl.BlockSpec((TOK_TILE, d), lambda i: (i, 0)),
            pl.BlockSpec((TOK_TILE, LANES), lambda i: (i, 0)),
            pl.BlockSpec((1, 1, 6, d), lambda i: (layer, mod_row(i), 0, 0)),
            pl.BlockSpec(memory_space=pl.ANY),
        ],
        out_specs=pl.BlockSpec((TOK_TILE, d), lambda i: (i, 0)),
        out_shape=jax.ShapeDtypeStruct((t, d), f32),
        input_output_aliases={1: 0},
        scratch_shapes=[pltpu.VMEM((TOP_K * TOK_TILE, d // 2), u32), pltpu.SemaphoreType.DMA(())],
        compiler_params=_cparams("arbitrary"),
        name="moe_combine",
    )(dest_tiles, xs, gates, mod, ypk)


def _moe_layer(xs, g, mod, layer, mod_row, w_r, b_r, w1, b1, w2, b2):
    t, d = xs.shape
    n_exp = w_r.shape[1]
    fpk, idx_o, gate_o, rank_o, cnt = _route(xs, g, mod, layer, mod_row, 3, 4, w_r, b_r)
    counts = cnt[0, :n_exp]
    padded = (counts + MOE_ROWS - 1) // MOE_ROWS * MOE_ROWS
    pad_end = jnp.cumsum(padded)
    pad_start = pad_end - padded
    dest = pad_start[idx_o[:, :TOP_K]] + rank_o[:, :TOP_K]
    n_rows = t * TOP_K + n_exp * MOE_ROWS
    n_blocks = n_rows // MOE_ROWS
    tok = jnp.broadcast_to(jnp.arange(t, dtype=i32)[:, None], (t, TOP_K))
    row_tok = jnp.zeros((n_rows,), i32).at[dest.reshape(-1)].set(tok.reshape(-1))
    block_expert = jnp.minimum(
        jnp.searchsorted(pad_end, jnp.arange(n_blocks, dtype=i32) * MOE_ROWS, side="right"),
        n_exp - 1).astype(i32)
    n_used = (pad_end[-1:] // MOE_ROWS).astype(i32)
    ypk = _experts(block_expert, n_used, row_tok, fpk, w1, b1, w2, b2)
    nt = t // TOK_TILE
    dest_tiles = dest.reshape(nt, TOK_TILE, TOP_K).transpose(0, 2, 1).reshape(nt, 1, TOP_K * TOK_TILE)
    return _combine(dest_tiles.astype(i32), xs, gate_o, mod, layer, mod_row, 5, ypk)


def kernel(x, c, ctx, c_ctx, ada_w, ada_b, norm_mix_g, norm_ffn_g, ev_w_in, ev_q_g, ev_k_g, ev_lq1, ev_lk1, ev_lq2, ev_lk2, ev_subln_g, ev_sgu_g, ev_w_s, ev_b_s, ev_w_out, od_w_in, od_conv_w, od_conv_b, od_w_a, od_b_a, od_w_i, od_b_i, od_lam, od_w_out, moe_w_r, moe_b_r, moe_w1, moe_b1, moe_w2, moe_b2):
    batch, n_lat, d = x.shape
    n_ctx = ctx.shape[1]
    depth = ada_w.shape[0]
    s_all = n_ctx + n_lat
    assert n_ctx % TOK_TILE == 0 and n_lat % TOK_TILE == 0 and n_lat % GRID_W == 0
    assert n_ctx % KV_CTX_TILE == 0 and n_lat % KV_TILE == 0
    assert od_w_a.shape[-1] == LRU_BLOCK and ev_w_s.shape[-1] == CHUNK
    assert d % (2 * HEAD_DIM) == 0 and batch < SUBLANES
    tiles_per_sample = s_all // TOK_TILE
    ctx_tiles = n_ctx // TOK_TILE
    aw = d // 2
    sw = d - aw
    lw = od_w_out.shape[1]

    def mod_row(i):
        return jnp.where(i % tiles_per_sample < ctx_tiles, batch, i // tiles_per_sample)

    cvec = jnp.zeros((SUBLANES, d), f32).at[:batch].set(c).at[batch].set(c_ctx)
    mod = _ada_all(cvec, ada_w, ada_b).reshape(depth, SUBLANES, 6, d)
    rope = _rope_tables(n_lat, n_ctx)
    xs = jnp.concatenate([ctx, x], axis=1).reshape(batch * s_all, d)

    for l in range(depth):
        h = _norm_mod(xs, norm_mix_g[l], mod, l, mod_row, 0, 1)
        if l % 2 == 0:
            e = l // 2
            lam_init = 0.8 - 0.6 * math.exp(-0.3 * l)
            proj = _matmul(h, ev_w_in[e].astype(bf16))
            q, k = _qk_norm_rope(proj, ev_q_g[e], ev_k_g[e], rope, tiles_per_sample, aw)
            lam_params = jnp.stack([ev_lq1[e], ev_lk1[e], ev_lq2[e], ev_lk2[e]]).astype(f32)
            o = _diff_attention(q, k, proj, 2 * aw // HEAD_DIM, lam_params, ev_subln_g[e],
                                batch, s_all, n_ctx, lam_init)
            sg = _sgu(proj, 3 * aw // sw, ev_sgu_g[e], ev_w_s[e], ev_b_s[e], sw)
            xs = _outproj_residual([o, sg], ev_w_out[e].astype(bf16), xs, mod, l, mod_row, 2)
        else:
            o_ = l // 2
            proj = _matmul(h, od_w_in[o_].astype(bf16))
            w_ai = jnp.concatenate([od_w_a[o_], od_w_i[o_]], axis=-1).astype(bf16)
            b_ai = jnp.concatenate([od_b_a[o_], od_b_i[o_]], axis=-1)[:, :, None, :].astype(f32)
            y = _lru(proj, od_conv_w[o_], od_conv_b[o_], w_ai, b_ai, od_lam[o_], batch, s_all, n_ctx, lw)
            xs = _outproj_residual([y], od_w_out[o_].astype(bf16), xs, mod, l, mod_row, 2)
        w2 = moe_w2[l].astype(bf16)
        w2x = jnp.stack([w2, jnp.zeros_like(w2)], axis=2).reshape(w2.shape[0], 2 * w2.shape[1], w2.shape[2])
        xs = _moe_layer(
            xs, norm_ffn_g[l], mod, l, mod_row, moe_w_r[l], moe_b_r[l],
            moe_w1[l].astype(bf16), moe_b1[l][:, None, :], w2x, moe_b2[l][:, None, :])
    return xs.reshape(batch, s_all, d)[:, n_ctx:, :]
```
